```python
import math
import jax, jax.numpy as jnp
from jax import lax
import numpy as np

D_MODEL = 1024
BATCH = 4
SEQ = 4096
DEPTH = 4
DEC_BATCH = 128
DEC_SEQ = 8
PAST_LEN = 8192
PAGE_SIZE = 128

HEAD_DIM = 64
SWA_WIDTH = D_MODEL // 2
N_Q_HEADS = SWA_WIDTH // HEAD_DIM
N_KV_HEADS = 2
Q_PER_KV = N_Q_HEADS // N_KV_HEADS
KV_WIDTH = N_KV_HEADS * HEAD_DIM
WINDOW = 128
ROPE_THETA = 500000.0
ROT_DIM = HEAD_DIM // 4
POOL_WINDOWS = (2, 4, 8, 16)
N_POOL_GROUPS = 4
POOL_WIDTH = D_MODEL // 4
POOL_GROUP = POOL_WIDTH // N_POOL_GROUPS
POOL_STATE = max(POOL_WINDOWS) - 1
N_MEM = 256
N_X_HEADS = 4
X_WIDTH = N_X_HEADS * HEAD_DIM
N_BRANCH = 3
IN_COLS = POOL_WIDTH + SWA_WIDTH + 2 * KV_WIDTH + X_WIDTH + N_BRANCH * D_MODEL
SPLITS = (POOL_WIDTH,
          POOL_WIDTH + SWA_WIDTH,
          POOL_WIDTH + SWA_WIDTH + KV_WIDTH,
          POOL_WIDTH + SWA_WIDTH + 2 * KV_WIDTH,
          POOL_WIDTH + SWA_WIDTH + 2 * KV_WIDTH + X_WIDTH)
D_FF = 256 * math.ceil(8 * D_MODEL / 3 / 256)
ALPHA = (2 * DEPTH) ** 0.25
BETA = (8 * DEPTH) ** -0.25
LN_EPS = 1e-5

kernel_name = "hybrid_pool_swa_cross_macaron_step"

F32 = jnp.float32


def _layer_norm(x, g, b):
    xf = x.astype(F32)
    mu = jnp.mean(xf, axis=-1, keepdims=True)
    var = jnp.mean(jnp.square(xf - mu), axis=-1, keepdims=True)
    return ((xf - mu) * lax.rsqrt(var + LN_EPS) * g.astype(F32) + b.astype(F32)).astype(x.dtype)


def _swiglu(x, w_gate, w_up, w_down):
    return (jax.nn.silu(x @ w_gate) * (x @ w_up)) @ w_down


def _rope_partial(x, pos):
    half = ROT_DIM // 2
    inv_freq = jnp.power(ROPE_THETA, -jnp.arange(half, dtype=F32) * (2.0 / ROT_DIM))
    ang = pos.astype(F32)[:, None] * inv_freq[None, :]
    cos = jnp.cos(ang)[:, None, :]
    sin = jnp.sin(ang)[:, None, :]
    xr = x[..., :ROT_DIM].astype(F32)
    x1, x2 = xr[..., :half], xr[..., half:]
    rot = jnp.concatenate([x1 * cos - x2 * sin, x2 * cos + x1 * sin], axis=-1).astype(x.dtype)
    return jnp.concatenate([rot, x[..., ROT_DIM:]], axis=-1)


def _sink_attention(q, k, v, sink, mask):
    s = jnp.einsum('...qhgd,...khd->...hgqk', q, k).astype(F32) * (HEAD_DIM ** -0.5)
    s = jnp.where(mask, s, -jnp.inf)
    sink_l = sink.astype(F32).reshape(N_KV_HEADS, Q_PER_KV)[:, :, None, None]
    m = jnp.maximum(jnp.max(s, axis=-1, keepdims=True), sink_l)
    p = jnp.exp(s - m)
    denom = jnp.sum(p, axis=-1, keepdims=True) + jnp.exp(sink_l - m)
    return jnp.einsum('...hgqk,...khd->...qhgd', (p / denom).astype(v.dtype), v)


def _swa_prompt(q, k, v, sink):
    b, s = q.shape[:2]
    nb = s // WINDOW
    qb = q.reshape(b, nb, WINDOW, N_KV_HEADS, Q_PER_KV, HEAD_DIM)
    kb = k.reshape(b, nb, WINDOW, N_KV_HEADS, HEAD_DIM)
    vb = v.reshape(b, nb, WINDOW, N_KV_HEADS, HEAD_DIM)
    pad = jnp.zeros_like(kb[:, :1])
    kk = jnp.concatenate([jnp.concatenate([pad, kb[:, :-1]], axis=1), kb], axis=2)
    vv = jnp.concatenate([jnp.concatenate([pad, vb[:, :-1]], axis=1), vb], axis=2)
    i = jnp.arange(WINDOW)[:, None]
    j = jnp.arange(2 * WINDOW)[None, :]
    rel = i + WINDOW - j
    blk = jnp.arange(nb)[:, None, None]
    mask = (rel >= 0) & (rel < WINDOW) & ((blk > 0) | (j >= WINDOW))
    o = _sink_attention(qb, kk, vv, sink, mask[None, :, None, None])
    return o.reshape(b, s, SWA_WIDTH)


def _swa_sample(q, k, v, cache_k, cache_v, sink, start):
    b, l = q.shape[:2]
    nbuf = cache_k.shape[1]
    kk = jnp.concatenate([cache_k, k], axis=1)
    vv = jnp.concatenate([cache_v, v], axis=1)
    qpos = start + jnp.arange(l)
    kpos = start - nbuf + jnp.arange(nbuf + l)
    rel = qpos[:, None] - kpos[None, :]
    mask = (rel >= 0) & (rel < WINDOW) & (kpos[None, :] >= 0)
    qg = q.reshape(b, l, N_KV_HEADS, Q_PER_KV, HEAD_DIM)
    o = _sink_attention(qg, kk, vv, sink, mask)
    return o.reshape(b, l, SWA_WIDTH), kk[:, -nbuf:], vv[:, -nbuf:]


def _pool_branch(xp, prefix, start, pool_mix, pool_scale):
    b, s, _ = xp.shape
    comb = jnp.concatenate([prefix, xp], axis=1)
    cs = jnp.cumsum(comb.astype(F32), axis=1)
    cs = jnp.concatenate([jnp.zeros((b, 1, POOL_WIDTH), F32), cs], axis=1)
    pos = start + jnp.arange(s)
    means = []
    for g, w in enumerate(POOL_WINDOWS):
        lo, hi = g * POOL_GROUP, (g + 1) * POOL_GROUP
        lead = cs[:, POOL_STATE + 1:POOL_STATE + 1 + s, lo:hi]
        lag = cs[:, POOL_STATE + 1 - w:POOL_STATE + 1 - w + s, lo:hi]
        cnt = jnp.minimum(w, pos + 1).astype(F32)[None, :, None]
        means.append((lead - lag) / cnt)
    mean = jnp.concatenate(means, axis=-1)
    d = (mean - xp.astype(F32)).astype(xp.dtype).reshape(b, s, N_POOL_GROUPS, POOL_GROUP)
    y = jnp.einsum('bsgc,gcd->bsgd', d, pool_mix).reshape(b, s, POOL_WIDTH) * pool_scale
    return y, comb[:, -POOL_STATE:]


def _cross_attention(q, mem_k, mem_v):
    b, s = q.shape[:2]
    sc = jnp.einsum('bshd,bmhd->bhsm', q, mem_k).astype(F32) * (HEAD_DIM ** -0.5)
    p = jax.nn.softmax(sc, axis=-1).astype(mem_v.dtype)
    return jnp.einsum('bhsm,bmhd->bshd', p, mem_v).reshape(b, s, X_WIDTH)


def _layer(x, start, pool_prefix, cache_k, cache_v, mem_k, mem_v,
           ffn1_w_gate, ffn1_w_up, ffn1_w_down, ln1_g, ln1_b,
           w_in, pool_mix, pool_scale, attn_sink, w_br_pool, w_br_swa, w_br_cross, w_o,
           ln2_g, ln2_b, ffn2_w_gate, ffn2_w_up, ffn2_w_down, ln3_g, ln3_b):
    b, s, _ = x.shape
    positions = start + jnp.arange(s)
    x = _layer_norm(ALPHA * x + 0.5 * _swiglu(x, ffn1_w_gate, ffn1_w_up, ffn1_w_down), ln1_g, ln1_b)
    h = x @ w_in
    xp, q, k, v, xq, gates = jnp.split(h, SPLITS, axis=-1)
    q = _rope_partial(q.reshape(b, s, N_Q_HEADS, HEAD_DIM), positions)
    k = _rope_partial(k.reshape(b, s, N_KV_HEADS, HEAD_DIM), positions)
    v = v.reshape(b, s, N_KV_HEADS, HEAD_DIM)
    y_pool, new_pool = _pool_branch(xp, pool_prefix, start, pool_mix, pool_scale)
    if cache_k is None:
        y_swa = _swa_prompt(q, k, v, attn_sink)
        nbuf = min(WINDOW, s)
        new_k, new_v = k[:, -nbuf:], v[:, -nbuf:]
    else:
        y_swa, new_k, new_v = _swa_sample(q, k, v, cache_k, cache_v, attn_sink, start)
    y_x = _cross_attention(xq.reshape(b, s, N_X_HEADS, HEAD_DIM), mem_k, mem_v)
    g = jax.nn.sigmoid(gates.reshape(b, s, N_BRANCH, D_MODEL))
    merged = (g[:, :, 0] * (y_pool @ w_br_pool)
              + g[:, :, 1] * (y_swa @ w_br_swa)
              + g[:, :, 2] * (y_x @ w_br_cross))
    x = _layer_norm(ALPHA * x + merged @ w_o, ln2_g, ln2_b)
    x = _layer_norm(ALPHA * x + 0.5 * _swiglu(x, ffn2_w_gate, ffn2_w_up, ffn2_w_down), ln3_g, ln3_b)
    return x, new_k, new_v, new_pool


def setup_inputs(seed: int = 0) -> dict:
    key = jax.random.key(seed)
    ks = iter(jax.random.split(key, 40))

    def nrm(shape, scale=1.0):
        return jax.random.normal(next(ks), shape, F32) * scale

    d = {}
    d["x_prompt"] = nrm((BATCH, SEQ, D_MODEL))
    d["x_sample"] = nrm((DEC_BATCH, DEC_SEQ, D_MODEL))
    d["cache_swa_k"] = nrm((DEPTH, DEC_BATCH, min(WINDOW, PAST_LEN), N_KV_HEADS, HEAD_DIM))
    d["cache_swa_v"] = nrm((DEPTH, DEC_BATCH, min(WINDOW, PAST_LEN), N_KV_HEADS, HEAD_DIM))
    d["state_pool"] = nrm((DEPTH, DEC_BATCH, POOL_STATE, POOL_WIDTH))
    d["cache_mem_k"] = nrm((DEPTH, DEC_BATCH, N_MEM, N_X_HEADS, HEAD_DIM))
    d["cache_mem_v"] = nrm((DEPTH, DEC_BATCH, N_MEM, N_X_HEADS, HEAD_DIM))
    d["mem_prompt"] = nrm((BATCH, N_MEM, D_MODEL))
    d["w_mem_k"] = nrm((DEPTH, D_MODEL, X_WIDTH), D_MODEL ** -0.5)
    d["w_mem_v"] = nrm((DEPTH, D_MODEL, X_WIDTH), D_MODEL ** -0.5)
    d["ffn1_w_gate"] = nrm((DEPTH, D_MODEL, D_FF), D_MODEL ** -0.5)
    d["ffn1_w_up"] = nrm((DEPTH, D_MODEL, D_FF), D_MODEL ** -0.5)
    d["ffn1_w_down"] = nrm((DEPTH, D_FF, D_MODEL), BETA * D_FF ** -0.5)
    d["ln1_g"] = 1.0 + nrm((DEPTH, D_MODEL), 0.02)
    d["ln1_b"] = nrm((DEPTH, D_MODEL), 0.02)
    d["w_in"] = nrm((DEPTH, D_MODEL, IN_COLS), D_MODEL ** -0.5)
    d["pool_mix"] = nrm((DEPTH, N_POOL_GROUPS, POOL_GROUP, POOL_GROUP), POOL_GROUP ** -0.5)
    d["pool_scale"] = 1.0 + nrm((DEPTH, POOL_WIDTH), 0.1)
    d["attn_sink"] = nrm((DEPTH, N_Q_HEADS), 0.5)
    d["w_br_pool"] = nrm((DEPTH, POOL_WIDTH, D_MODEL), POOL_WIDTH ** -0.5)
    d["w_br_swa"] = nrm((DEPTH, SWA_WIDTH, D_MODEL), SWA_WIDTH ** -0.5)
    d["w_br_cross"] = nrm((DEPTH, X_WIDTH, D_MODEL), X_WIDTH ** -0.5)
    d["w_o"] = nrm((DEPTH, D_MODEL, D_MODEL), BETA * D_MODEL ** -0.5)
    d["ln2_g"] = 1.0 + nrm((DEPTH, D_MODEL), 0.02)
    d["ln2_b"] = nrm((DEPTH, D_MODEL), 0.02)
    d["ffn2_w_gate"] = nrm((DEPTH, D_MODEL, D_FF), D_MODEL ** -0.5)
    d["ffn2_w_up"] = nrm((DEPTH, D_MODEL, D_FF), D_MODEL ** -0.5)
    d["ffn2_w_down"] = nrm((DEPTH, D_FF, D_MODEL), BETA * D_FF ** -0.5)
    d["ln3_g"] = 1.0 + nrm((DEPTH, D_MODEL), 0.02)
    d["ln3_b"] = nrm((DEPTH, D_MODEL), 0.02)
    return d


def reference(x_prompt, x_sample, cache_swa_k, cache_swa_v, state_pool, cache_mem_k, cache_mem_v,
              mem_prompt, w_mem_k, w_mem_v,
              ffn1_w_gate, ffn1_w_up, ffn1_w_down, ln1_g, ln1_b,
              w_in, pool_mix, pool_scale, attn_sink, w_br_pool, w_br_swa, w_br_cross, w_o,
              ln2_g, ln2_b, ffn2_w_gate, ffn2_w_up, ffn2_w_down, ln3_g, ln3_b):
    yp, ys = x_prompt, x_sample
    bp = x_prompt.shape[0]
    pk, pv, pp, pmk, pmv = [], [], [], [], []
    sk, sv, sp = [], [], []
    for l in range(DEPTH):
        lw = (ffn1_w_gate[l], ffn1_w_up[l], ffn1_w_down[l], ln1_g[l], ln1_b[l],
              w_in[l], pool_mix[l], pool_scale[l], attn_sink[l], w_br_pool[l], w_br_swa[l],
              w_br_cross[l], w_o[l], ln2_g[l], ln2_b[l],
              ffn2_w_gate[l], ffn2_w_up[l], ffn2_w_down[l], ln3_g[l], ln3_b[l])
        mk = (mem_prompt @ w_mem_k[l]).reshape(bp, N_MEM, N_X_HEADS, HEAD_DIM)
        mv = (mem_prompt @ w_mem_v[l]).reshape(bp, N_MEM, N_X_HEADS, HEAD_DIM)
        prefix0 = jnp.zeros((bp, POOL_STATE, POOL_WIDTH), yp.dtype)
        yp, nk, nv, npool = _layer(yp, 0, prefix0, None, None, mk, mv, *lw)
        pk.append(nk); pv.append(nv); pp.append(npool); pmk.append(mk); pmv.append(mv)
        ys, nk, nv, npool = _layer(ys, PAST_LEN, state_pool[l], cache_swa_k[l], cache_swa_v[l],
                                   cache_mem_k[l], cache_mem_v[l], *lw)
        sk.append(nk); sv.append(nv); sp.append(npool)
    return (yp, ys,
            jnp.stack(pk), jnp.stack(pv), jnp.stack(pp), jnp.stack(pmk), jnp.stack(pmv),
            jnp.stack(sk), jnp.stack(sv), jnp.stack(sp))
```

```python
import functools

import jax
import jax.numpy as jnp
from jax import lax
from jax.experimental import pallas as pl
from jax.experimental.pallas import tpu as pltpu

F32 = jnp.float32
BF16 = jnp.bfloat16

HEAD_DIM = 64
N_Q_HEADS = 8
N_KV_HEADS = 2
Q_PER_KV = N_Q_HEADS // N_KV_HEADS
N_X_HEADS = 4
WINDOW = 128
ROPE_THETA = 500000.0
ROT_DIM = 16
ROT_HALF = ROT_DIM // 2
POOL_GROUP = 64
POOL_HALO = 16
PAST_LEN = 8192
LN_EPS = 1e-5
SM_SCALE = HEAD_DIM ** -0.5

VMEM_LIMIT_BYTES_V7X = 56 * 1024 * 1024
LANES = 128

TOKEN_TILE = 512
FF_CHUNK = 256
SAMPLE_SEQS = 8


def _cparams(n_axes):
    return pltpu.CompilerParams(dimension_semantics=("parallel",) * n_axes,
                                vmem_limit_bytes=VMEM_LIMIT_BYTES_V7X)


def _resident(shape, index_map):
    return pl.BlockSpec(shape, index_map, pipeline_mode=pl.Buffered(1))


def _dot(a, b):
    return jnp.dot(a, b, preferred_element_type=F32)


def _dot_nt(a, b):
    return lax.dot_general(a, b, (((1,), (1,)), ((), ())), preferred_element_type=F32)


def _layer_norm(z, g, b):
    mu = jnp.mean(z, axis=-1, keepdims=True)
    zc = z - mu
    var = jnp.mean(zc * zc, axis=-1, keepdims=True)
    return zc * lax.rsqrt(var + LN_EPS) * g + b


def _rope_table_kernel(cos_ref, sp_ref, sm_ref, *, seq, dec_seq):
    n = cos_ref.shape[0]
    r = lax.broadcasted_iota(jnp.int32, (n, LANES), 0)
    lane = lax.broadcasted_iota(jnp.int32, (n, LANES), 1)
    pos = jnp.where(r < seq, r, PAST_LEN + (r - seq) % dec_seq)
    d = lane % HEAD_DIM
    fidx = (d % ROT_HALF).astype(F32)
    inv_freq = jnp.power(ROPE_THETA, -fidx * (2.0 / ROT_DIM))
    ang = pos.astype(F32) * inv_freq
    cos = jnp.cos(ang)
    sin = jnp.sin(ang)
    cos_ref[...] = jnp.where(d < ROT_DIM, cos, 1.0)
    sp_ref[...] = jnp.where((d >= ROT_HALF) & (d < ROT_DIM), sin, 0.0)
    sm_ref[...] = jnp.where(d < ROT_HALF, -sin, 0.0)


def _rope_tables(seq, n_sample_rows, dec_seq):
    n = seq + n_sample_rows
    out = jax.ShapeDtypeStruct((n, LANES), F32)
    return pl.pallas_call(
        functools.partial(_rope_table_kernel, seq=seq, dec_seq=dec_seq),
        out_shape=(out, out, out),
        name="rope_tables",
    )()


def _mem_kv_kernel(m_ref, wk_ref, wv_ref, k_ref, v_ref):
    m = m_ref[...]
    k_ref[...] = _dot(m, wk_ref[...])
    v_ref[...] = _dot(m, wv_ref[...])


def _mem_kv(mem, w_k, w_v):
    depth, d, xw = w_k.shape
    rows = mem.shape[0]
    out = jax.ShapeDtypeStruct((depth, rows, xw), F32)
    w_spec = pl.BlockSpec((None, d, xw), lambda l: (l, 0, 0))
    o_spec = pl.BlockSpec((None, rows, xw), lambda l: (l, 0, 0))
    return pl.pallas_call(
        _mem_kv_kernel,
        grid=(depth,),
        in_specs=[pl.BlockSpec((rows, d), lambda l: (0, 0)), w_spec, w_spec],
        out_specs=(o_spec, o_spec),
        out_shape=(out, out),
        compiler_params=_cparams(1),
        name="mem_kv",
    )(mem, w_k, w_v)


def _ffn_ln_kernel(x_ref, wg_ref, wu_ref, wd_ref, g_ref, b_ref, o_ref, a_ref, *, alpha):
    x = x_ref[...]
    xb = x.astype(BF16)
    d_ff = wg_ref.shape[1]
    for c in range(0, d_ff, FF_CHUNK):
        g = _dot(xb, wg_ref[:, c:c + FF_CHUNK])
        u = _dot(xb, wu_ref[:, c:c + FF_CHUNK])
        a_ref[:, c:c + FF_CHUNK] = (g * jax.nn.sigmoid(g) * u).astype(BF16)
    y = _dot(a_ref[...], wd_ref[...])
    o_ref[...] = _layer_norm(alpha * x + 0.5 * y, g_ref[...], b_ref[...])


def _ffn_ln(x, w_gate, w_up, w_down, ln_g, ln_b, layer, alpha):
    n, d = x.shape
    d_ff = w_gate.shape[2]
    tm = TOKEN_TILE
    lsel = lambda i: (layer, 0, 0)
    return pl.pallas_call(
        functools.partial(_ffn_ln_kernel, alpha=alpha),
        grid=(n // tm,),
        in_specs=[
            pl.BlockSpec((tm, d), lambda i: (i, 0)),
            _resident((None, d, d_ff), lsel),
            _resident((None, d, d_ff), lsel),
            _resident((None, d_ff, d), lsel),
            _resident((None, 1, d), lsel),
            _resident((None, 1, d), lsel),
        ],
        out_specs=pl.BlockSpec((tm, d), lambda i: (i, 0)),
        out_shape=jax.ShapeDtypeStruct((n, d), F32),
        scratch_shapes=[pltpu.VMEM((tm, d_ff), BF16)],
        compiler_params=_cparams(1),
        name="ffn_ln",
    )(x, w_gate, w_up, w_down, ln_g, ln_b)


def _proj_kernel(x_ref, w_ref, cos_ref, sp_ref, sm_ref, xp_ref, q_ref, k_ref, v_ref, xq_ref,
                 *, pool_w, swa_w, kv_w, x_w):
    xb = x_ref[...].astype(BF16)
    cos, sp, sm = cos_ref[...], sp_ref[...], sm_ref[...]

    def rope(t):
        return t * cos + pltpu.roll(t, ROT_HALF, 1) * sp + pltpu.roll(t, LANES - ROT_HALF, 1) * sm

    c0 = 0
    xp_ref[...] = _dot(xb, w_ref[:, c0:c0 + pool_w])
    c0 += pool_w
    for s in range(swa_w // LANES):
        q = _dot(xb, w_ref[:, c0 + s * LANES:c0 + (s + 1) * LANES])
        q_ref[:, s * LANES:(s + 1) * LANES] = rope(q) * SM_SCALE
    c0 += swa_w
    k_ref[...] = rope(_dot(xb, w_ref[:, c0:c0 + kv_w]))
    c0 += kv_w
    v_ref[...] = _dot(xb, w_ref[:, c0:c0 + kv_w])
    c0 += kv_w
    xq_ref[...] = _dot(xb, w_ref[:, c0:c0 + x_w]) * SM_SCALE


def _proj(x, w_in_p, tables, layer, n_prompt, seq, widths):
    n, d = x.shape
    pool_w, swa_w, kv_w, x_w = widths
    assert kv_w == LANES
    tm = TOKEN_TILE
    cols = w_in_p.shape[2]
    n_prompt_tiles, seq_tiles = n_prompt // tm, seq // tm

    def table_idx(i):
        return (jnp.where(i < n_prompt_tiles, i % seq_tiles, seq_tiles + i - n_prompt_tiles), 0)

    t_spec = pl.BlockSpec((tm, LANES), table_idx)
    row = lambda w: pl.BlockSpec((tm, w), lambda i: (i, 0))
    sds = lambda w: jax.ShapeDtypeStruct((n, w), F32)
    return pl.pallas_call(
        functools.partial(_proj_kernel, pool_w=pool_w, swa_w=swa_w, kv_w=kv_w, x_w=x_w),
        grid=(n // tm,),
        in_specs=[row(d), _resident((None, d, cols), lambda i: (layer, 0, 0)), t_spec, t_spec, t_spec],
        out_specs=(row(pool_w), row(swa_w), row(kv_w), row(kv_w), row(x_w)),
        out_shape=(sds(pool_w), sds(swa_w), sds(kv_w), sds(kv_w), sds(x_w)),
        compiler_params=_cparams(1),
        name="proj",
    )(x, w_in_p, *tables)


def _pool_window_sums(e_ref, w2_ref, w4_ref, w8_ref, n):
    h = 2 * POOL_HALO
    lead = (slice(None),) * (len(e_ref.shape) - 2)
    rows = lambda a, b: lead + (slice(a, b), slice(None))
    sl = lambda ref, a, b: ref[rows(a, b)]
    w2_ref[rows(8, n)] = sl(e_ref, 8, n) + sl(e_ref, 7, n - 1)
    w4_ref[rows(16, n)] = sl(w2_ref, 16, n) + sl(w2_ref, 14, n - 2)
    w8_ref[rows(24, n)] = sl(w4_ref, 24, n) + sl(w4_ref, 20, n - 4)
    w16 = sl(w8_ref, h, n) + sl(w8_ref, h - 8, n - 8)
    return sl(w2_ref, h, n), sl(w4_ref, h, n), sl(w8_ref, h, n), w16


def _pool_delta(sums, xp, pos):
    w2, w4, w8, w16 = sums
    grp = lax.broadcasted_iota(jnp.int32, xp.shape, xp.ndim - 1) // POOL_GROUP
    s = jnp.where(grp == 0, w2, jnp.where(grp == 1, w4, jnp.where(grp == 2, w8, w16)))
    width = jnp.left_shift(2, grp)
    cnt = jnp.minimum(width, pos + 1).astype(F32)
    return s / cnt - xp


def _band_mask(n_rows, n_keys, q_period):
    qi = lax.broadcasted_iota(jnp.int32, (n_rows, n_keys), 0) % q_period
    kj = lax.broadcasted_iota(jnp.int32, (n_rows, n_keys), 1)
    return kj, (kj > qi) & (kj <= qi + WINDOW)


def _sink_softmax(s, mask, sink):
    s = jnp.where(mask, s, -jnp.inf)
    m = jnp.maximum(jnp.max(s, axis=-1, keepdims=True), sink)
    p = jnp.exp(s - m)
    den = jnp.sum(p, axis=-1, keepdims=True) + jnp.exp(sink - m)
    return p * (1.0 / den)


def _softmax(s):
    m = jnp.max(s, axis=-1, keepdims=True)
    p = jnp.exp(s - m)
    return p * (1.0 / jnp.sum(p, axis=-1, keepdims=True))


def _mix_prompt_kernel(sink_ref, xp_ref, xph_ref, q_ref, k_ref, kh_ref, v_ref, vh_ref, xq_ref,
                       mk_ref, mv_ref, pbd_ref, pscale_ref, y_ref,
                       e_ref, w2_ref, w4_ref, w8_ref, kb_ref, vb_ref, *, layer):
    t = pl.program_id(1)
    tq = xp_ref.shape[0]
    pool_w = xp_ref.shape[1]
    swa_w = q_ref.shape[1]

    xp = xp_ref[...]
    e_ref[0:POOL_HALO, :] = jnp.zeros((POOL_HALO, pool_w), F32)
    e_ref[POOL_HALO:2 * POOL_HALO, :] = jnp.where(t == 0, 0.0, xph_ref[...])
    e_ref[2 * POOL_HALO:, :] = xp
    sums = _pool_window_sums(e_ref, w2_ref, w4_ref, w8_ref, tq + 2 * POOL_HALO)
    pos = t * tq + lax.broadcasted_iota(jnp.int32, (tq, pool_w), 0)
    delta = _pool_delta(sums, xp, pos)
    y_pool = _dot(delta.astype(BF16), pbd_ref[...]) * pscale_ref[...]
    y_ref[:, 0:pool_w] = y_pool.astype(BF16)

    for g in range(N_KV_HEADS):
        cs = slice(g * HEAD_DIM, (g + 1) * HEAD_DIM)
        kb_ref[g, 0:WINDOW, :] = kh_ref[:, cs].astype(BF16)
        kb_ref[g, WINDOW:, :] = k_ref[:, cs].astype(BF16)
        vb_ref[g, 0:WINDOW, :] = vh_ref[:, cs].astype(BF16)
        vb_ref[g, WINDOW:, :] = v_ref[:, cs].astype(BF16)
    kj, band = _band_mask(WINDOW, 2 * WINDOW, WINDOW)
    first_key = jnp.where(t == 0, WINDOW, 0)
    band_first = band & (kj >= first_key)
    for j in range(tq // WINDOW):
        mask = band_first if j == 0 else band
        rows = slice(j * WINDOW, (j + 1) * WINDOW)
        keys = slice(j * WINDOW, (j + 2) * WINDOW)
        outs = []
        for h in range(N_Q_HEADS):
            g = h // Q_PER_KV
            qh = q_ref[rows, h * HEAD_DIM:(h + 1) * HEAD_DIM].astype(BF16)
            p = _sink_softmax(_dot_nt(qh, kb_ref[g, keys, :]), mask, sink_ref[layer, h])
            outs.append(_dot(p.astype(BF16), vb_ref[g, keys, :]))
        y_ref[rows, pool_w:pool_w + swa_w] = jnp.concatenate(outs, axis=1).astype(BF16)

    outs = []
    for h in range(N_X_HEADS):
        cs = slice(h * HEAD_DIM, (h + 1) * HEAD_DIM)
        p = _softmax(_dot_nt(xq_ref[:, cs].astype(BF16), mk_ref[:, cs].astype(BF16)))
        outs.append(_dot(p.astype(BF16), mv_ref[:, cs].astype(BF16)))
    y_ref[:, pool_w + swa_w:] = jnp.concatenate(outs, axis=1).astype(BF16)


def _mix_prompt(sink, xp, q, k, v, xq, mk, mv, pool_bd, pool_scale, layer, batch, seq):
    tq = TOKEN_TILE
    n_t = seq // tq
    pool_w, swa_w, kv_w, x_w = xp.shape[1], q.shape[1], k.shape[1], xq.shape[1]
    n_mem = mk.shape[1] // batch
    d_out = pool_w + swa_w + x_w

    tile = lambda w: pl.BlockSpec((tq, w), lambda b, t: (b * n_t + t, 0))

    def halo(rows, w):
        per_tile = tq // rows
        return pl.BlockSpec((rows, w), lambda b, t: (jnp.maximum((b * n_t + t) * per_tile - 1, 0), 0))

    mem = pl.BlockSpec((None, n_mem, x_w), lambda b, t: (layer, b, 0))
    return pl.pallas_call(
        functools.partial(_mix_prompt_kernel, layer=layer),
        grid=(batch, n_t),
        in_specs=[
            pl.BlockSpec(memory_space=pltpu.SMEM),
            tile(pool_w), halo(POOL_HALO, pool_w),
            tile(swa_w),
            tile(kv_w), halo(WINDOW, kv_w),
            tile(kv_w), halo(WINDOW, kv_w),
            tile(x_w),
            mem, mem,
            _resident((None, pool_w, pool_w), lambda b, t: (layer, 0, 0)),
            _resident((None, 1, pool_w), lambda b, t: (layer, 0, 0)),
        ],
        out_specs=pl.BlockSpec((tq, d_out), lambda b, t: (b * n_t + t, 0)),
        out_shape=jax.ShapeDtypeStruct((batch * seq, d_out), BF16),
        scratch_shapes=[
            pltpu.VMEM((tq + 2 * POOL_HALO, pool_w), F32),
            pltpu.VMEM((tq + 2 * POOL_HALO, pool_w), F32),
            pltpu.VMEM((tq + 2 * POOL_HALO, pool_w), F32),
            pltpu.VMEM((tq + 2 * POOL_HALO, pool_w), F32),
            pltpu.VMEM((N_KV_HEADS, tq + WINDOW, HEAD_DIM), BF16),
            pltpu.VMEM((N_KV_HEADS, tq + WINDOW, HEAD_DIM), BF16),
        ],
        compiler_params=_cparams(2),
        name="mix_prompt",
    )(sink, xp, xp, q, k, k, v, v, xq, mk, mv, pool_bd, pool_scale)


def _mix_sample_kernel(sink_ref, xp_ref, q_ref, k_ref, v_ref, xq_ref, ck_ref, cv_ref, st_ref,
                       cmk_ref, cmv_ref, pbd_ref, pscale_ref,
                       y_ref, nk_ref, nv_ref, npool_ref,
                       e_ref, w2_ref, w4_ref, w8_ref, kb_ref, vb_ref, ys_ref, *, layer, dec_seq):
    sb = ck_ref.shape[0]
    pool_w = xp_ref.shape[1]
    swa_w = q_ref.shape[1]
    n_keys = kb_ref.shape[1]
    h2 = 2 * POOL_HALO

    xp = xp_ref[...].reshape(sb, dec_seq, pool_w)
    e_ref[:, 0:POOL_HALO, :] = jnp.zeros((sb, POOL_HALO, pool_w), F32)
    e_ref[:, POOL_HALO:h2, :] = st_ref[...]
    e_ref[:, h2:, :] = xp
    sums = _pool_window_sums(e_ref, w2_ref, w4_ref, w8_ref, h2 + dec_seq)
    pos = PAST_LEN + lax.broadcasted_iota(jnp.int32, xp.shape, 1)
    delta = _pool_delta(sums, xp, pos).reshape(sb * dec_seq, pool_w)
    ys_ref[:, 0:pool_w] = _dot(delta.astype(BF16), pbd_ref[...]) * pscale_ref[...]
    npool_ref[...] = e_ref[:, h2 + dec_seq - POOL_HALO:, :]

    kb_ref[...] = jnp.zeros(kb_ref.shape, BF16)
    vb_ref[...] = jnp.zeros(vb_ref.shape, BF16)
    n_rows = Q_PER_KV * dec_seq
    _, band = _band_mask(n_rows, n_keys, dec_seq)
    head_of_row = lax.broadcasted_iota(jnp.int32, (n_rows, 1), 0) // dec_seq

    def per_sequence(s, carry):
        r = pl.multiple_of(s * dec_seq, dec_seq)
        rows = pl.ds(r, dec_seq)
        k_new = k_ref[rows, :]
        v_new = v_ref[rows, :]
        ck = ck_ref[s]
        cv = cv_ref[s]
        nk_ref[s, 0:WINDOW - dec_seq, :] = ck[dec_seq:, :]
        nk_ref[s, WINDOW - dec_seq:, :] = k_new
        nv_ref[s, 0:WINDOW - dec_seq, :] = cv[dec_seq:, :]
        nv_ref[s, WINDOW - dec_seq:, :] = v_new
        for g in range(N_KV_HEADS):
            cs = slice(g * HEAD_DIM, (g + 1) * HEAD_DIM)
            kb_ref[g, 0:WINDOW, :] = ck[:, cs].astype(BF16)
            kb_ref[g, WINDOW:WINDOW + 2 * dec_seq, :] = jnp.concatenate(
                [k_new[:, cs], jnp.zeros((dec_seq, HEAD_DIM), F32)], axis=0).astype(BF16)
            vb_ref[g, 0:WINDOW, :] = cv[:, cs].astype(BF16)
            vb_ref[g, WINDOW:WINDOW + 2 * dec_seq, :] = jnp.concatenate(
                [v_new[:, cs], jnp.zeros((dec_seq, HEAD_DIM), F32)], axis=0).astype(BF16)
        q = q_ref[rows, :]
        outs = []
        for g in range(N_KV_HEADS):
            heads = range(g * Q_PER_KV, (g + 1) * Q_PER_KV)
            qg = jnp.concatenate([q[:, h * HEAD_DIM:(h + 1) * HEAD_DIM] for h in heads], axis=0)
            sink = sink_ref[layer, g * Q_PER_KV + Q_PER_KV - 1]
            for i in reversed(range(Q_PER_KV - 1)):
                sink = jnp.where(head_of_row <= i, sink_ref[layer, g * Q_PER_KV + i], sink)
            p = _sink_softmax(_dot_nt(qg.astype(BF16), kb_ref[g]), band, sink)
            o = _dot(p.astype(BF16), vb_ref[g])
            outs.extend(o[i * dec_seq:(i + 1) * dec_seq, :] for i in range(Q_PER_KV))
        ys_ref[rows, pool_w:pool_w + swa_w] = jnp.concatenate(outs, axis=1)

        xq = xq_ref[rows, :]
        cmk = cmk_ref[s]
        cmv = cmv_ref[s]
        outs = []
        for h in range(N_X_HEADS):
            cs = slice(h * HEAD_DIM, (h + 1) * HEAD_DIM)
            p = _softmax(_dot_nt(xq[:, cs].astype(BF16), cmk[:, cs].astype(BF16)))
            outs.append(_dot(p.astype(BF16), cmv[:, cs].astype(BF16)))
        ys_ref[rows, pool_w + swa_w:] = jnp.concatenate(outs, axis=1)
        return carry

    lax.fori_loop(0, sb, per_sequence, 0)
    y_ref[...] = ys_ref[...].astype(BF16)


def _mix_sample(sink, xp, q, k, v, xq, cache_k, cache_v, state16, cache_mk, cache_mv, pool_bd, pool_scale,
                layer, n_prompt, dec_seq):
    dec_batch = cache_k.shape[1]
    sb = SAMPLE_SEQS
    rows = sb * dec_seq
    off = n_prompt // rows
    pool_w, swa_w, kv_w, x_w = xp.shape[1], q.shape[1], k.shape[1], xq.shape[1]
    n_mem = cache_mk.shape[2]
    d_out = pool_w + swa_w + x_w
    n_keys = 2 * WINDOW

    tok = lambda w: pl.BlockSpec((rows, w), lambda i: (off + i, 0))
    per_seq = lambda a, w: pl.BlockSpec((None, sb, a, w), lambda i: (layer, i, 0, 0))
    out_seq = lambda a, w: pl.BlockSpec((sb, a, w), lambda i: (i, 0, 0))
    e_shape = (sb, 2 * POOL_HALO + dec_seq, pool_w)
    return pl.pallas_call(
        functools.partial(_mix_sample_kernel, layer=layer, dec_seq=dec_seq),
        grid=(dec_batch // sb,),
        in_specs=[
            pl.BlockSpec(memory_space=pltpu.SMEM),
            tok(pool_w), tok(swa_w), tok(kv_w), tok(kv_w), tok(x_w),
            per_seq(WINDOW, kv_w), per_seq(WINDOW, kv_w), per_seq(POOL_HALO, pool_w),
            per_seq(n_mem, x_w), per_seq(n_mem, x_w),
            _resident((None, pool_w, pool_w), lambda i: (layer, 0, 0)),
            _resident((None, 1, pool_w), lambda i: (layer, 0, 0)),
        ],
        out_specs=(
            pl.BlockSpec((rows, d_out), lambda i: (i, 0)),
            out_seq(WINDOW, kv_w), out_seq(WINDOW, kv_w), out_seq(POOL_HALO, pool_w),
        ),
        out_shape=(
            jax.ShapeDtypeStruct((dec_batch * dec_seq, d_out), BF16),
            jax.ShapeDtypeStruct((dec_batch, WINDOW, kv_w), F32),
            jax.ShapeDtypeStruct((dec_batch, WINDOW, kv_w), F32),
            jax.ShapeDtypeStruct((dec_batch, POOL_HALO, pool_w), F32),
        ),
        scratch_shapes=[
            pltpu.VMEM(e_shape, F32), pltpu.VMEM(e_shape, F32), pltpu.VMEM(e_shape, F32), pltpu.VMEM(e_shape, F32),
            pltpu.VMEM((N_KV_HEADS, n_keys, HEAD_DIM), BF16),
            pltpu.VMEM((N_KV_HEADS, n_keys, HEAD_DIM), BF16),
            pltpu.VMEM((rows, d_out), F32),
        ],
        compiler_params=_cparams(1),
        name="mix_sample",
    )(sink, xp, q, k, v, xq, cache_k, cache_v, state16, cache_mk, cache_mv, pool_bd, pool_scale)


def _merge_ln_kernel(x_ref, y_ref, wg_ref, wbp_ref, wbs_ref, wbx_ref, wo_ref, g_ref, b_ref, o_ref, *, alpha):
    x = x_ref[...]
    xb = x.astype(BF16)
    d = x.shape[1]
    c0 = 0
    merged = None
    for b, w_ref in enumerate((wbp_ref, wbs_ref, wbx_ref)):
        w = w_ref.shape[0]
        gate = jax.nn.sigmoid(_dot(xb, wg_ref[:, b * d:(b + 1) * d]))
        term = gate * _dot(y_ref[:, c0:c0 + w], w_ref[...])
        merged = term if merged is None else merged + term
        c0 += w
    z = alpha * x + _dot(merged.astype(BF16), wo_ref[...])
    o_ref[...] = _layer_norm(z, g_ref[...], b_ref[...])


def _merge_ln(x, y, w_in_g, w_br_pool, w_br_swa, w_br_cross, w_o, ln_g, ln_b, layer, alpha):
    n, d = x.shape
    tm = TOKEN_TILE
    lsel = lambda i: (layer, 0, 0)
    res = lambda w: _resident((None,) + w.shape[1:], lsel)
    return pl.pallas_call(
        functools.partial(_merge_ln_kernel, alpha=alpha),
        grid=(n // tm,),
        in_specs=[
            pl.BlockSpec((tm, d), lambda i: (i, 0)),
            pl.BlockSpec((tm, y.shape[1]), lambda i: (i, 0)),
            res(w_in_g), res(w_br_pool), res(w_br_swa), res(w_br_cross), res(w_o), res(ln_g), res(ln_b),
        ],
        out_specs=pl.BlockSpec((tm, d), lambda i: (i, 0)),
        out_shape=jax.ShapeDtypeStruct((n, d), F32),
        compiler_params=_cparams(1),
        name="merge_ln",
    )(x, y, w_in_g, w_br_pool, w_br_swa, w_br_cross, w_o, ln_g, ln_b)


def kernel(x_prompt, x_sample, cache_swa_k, cache_swa_v, state_pool, cache_mem_k, cache_mem_v, mem_prompt, w_mem_k, w_mem_v, ffn1_w_gate, ffn1_w_up, ffn1_w_down, ln1_g, ln1_b, w_in, pool_mix, pool_scale, attn_sink, w_br_pool, w_br_swa, w_br_cross, w_o, ln2_g, ln2_b, ffn2_w_gate, ffn2_w_up, ffn2_w_down, ln3_g, ln3_b):
    batch, seq, d = x_prompt.shape
    dec_batch, dec_seq, _ = x_sample.shape
    depth = w_in.shape[0]
    n_groups, pool_group = pool_mix.shape[1], pool_mix.shape[2]
    pool_w = n_groups * pool_group
    swa_w = w_br_swa.shape[1]
    kv_w = cache_swa_k.shape[3] * cache_swa_k.shape[4]
    x_w = w_br_cross.shape[1]
    n_mem = mem_prompt.shape[1]
    proj_w = pool_w + swa_w + 2 * kv_w + x_w
    assert pool_group == POOL_GROUP and swa_w == N_Q_HEADS * HEAD_DIM and kv_w == N_KV_HEADS * HEAD_DIM
    assert x_w == N_X_HEADS * HEAD_DIM and cache_swa_k.shape[2] == WINDOW
    assert state_pool.shape[2] == POOL_HALO - 1 and w_in.shape[2] == proj_w + 3 * d
    alpha = (2 * depth) ** 0.25
    n_prompt, n_sample = batch * seq, dec_batch * dec_seq

    bf = lambda w: w.astype(BF16)
    vec = lambda p: p.reshape(depth, 1, p.shape[-1])
    w_in_p, w_in_g = bf(w_in[:, :, :proj_w]), bf(w_in[:, :, proj_w:])
    pool_bd = bf(jnp.einsum("lgcd,gh->lgchd", pool_mix, jnp.eye(n_groups, dtype=F32))
                 .reshape(depth, pool_w, pool_w))
    state16 = jnp.pad(state_pool, ((0, 0), (0, 0), (1, 0), (0, 0)))
    cache_k = cache_swa_k.reshape(depth, dec_batch, WINDOW, kv_w)
    cache_v = cache_swa_v.reshape(depth, dec_batch, WINDOW, kv_w)
    cache_mk = cache_mem_k.reshape(depth, dec_batch, n_mem, x_w)
    cache_mv = cache_mem_v.reshape(depth, dec_batch, n_mem, x_w)

    tables = _rope_tables(seq, n_sample, dec_seq)
    mk, mv = _mem_kv(bf(mem_prompt.reshape(batch * n_mem, d)), bf(w_mem_k), bf(w_mem_v))

    ffn1 = (bf(ffn1_w_gate), bf(ffn1_w_up), bf(ffn1_w_down), vec(ln1_g), vec(ln1_b))
    ffn2 = (bf(ffn2_w_gate), bf(ffn2_w_up), bf(ffn2_w_down), vec(ln3_g), vec(ln3_b))
    merge = (w_in_g, bf(w_br_pool), bf(w_br_swa), bf(w_br_cross), bf(w_o), vec(ln2_g), vec(ln2_b))
    pscale = vec(pool_scale)

    x = jnp.concatenate([x_prompt.reshape(n_prompt, d), x_sample.reshape(n_sample, d)], axis=0)
    pk, pv, pp, sk, sv, sp = [], [], [], [], [], []
    for l in range(depth):
        x = _ffn_ln(x, *ffn1, l, alpha)
        xp, q, k, v, xq = _proj(x, w_in_p, tables, l, n_prompt, seq, (pool_w, swa_w, kv_w, x_w))
        y_p = _mix_prompt(attn_sink, xp, q, k, v, xq, mk, mv, pool_bd, pscale, l, batch, seq)
        y_s, nk, nv, npool = _mix_sample(attn_sink, xp, q, k, v, xq, cache_k, cache_v, state16,
                                         cache_mk, cache_mv, pool_bd, pscale, l, n_prompt, dec_seq)
        x = _merge_ln(x, jnp.concatenate([y_p, y_s], axis=0), *merge, l, alpha)
        x = _ffn_ln(x, *ffn2, l, alpha)
        last = lambda a, rows: a[:n_prompt].reshape(batch, seq, a.shape[1])[:, seq - rows:]
        pk.append(last(k, WINDOW))
        pv.append(last(v, WINDOW))
        pp.append(last(xp, POOL_HALO - 1))
        sk.append(nk)
        sv.append(nv)
        sp.append(npool[:, 1:])

    heads = lambda a, n_h: a.reshape(a.shape[:-1] + (n_h, HEAD_DIM))
    return (x[:n_prompt].reshape(batch, seq, d), x[n_prompt:].reshape(dec_batch, dec_seq, d),
            heads(jnp.stack(pk), N_KV_HEADS), heads(jnp.stack(pv), N_KV_HEADS), jnp.stack(pp),
            heads(mk.reshape(depth, batch, n_mem, x_w), N_X_HEADS),
            heads(mv.reshape(depth, batch, n_mem, x_w), N_X_HEADS),
            heads(jnp.stack(sk), N_KV_HEADS), heads(jnp.stack(sv), N_KV_HEADS), jnp.stack(sp))
```

```python
import functools

import jax
import jax.numpy as jnp
from jax import lax
from jax.experimental import pallas as pl
from jax.experimental.pallas import tpu as pltpu

F32 = jnp.float32
BF16 = jnp.bfloat16

HEAD_DIM = 64
N_Q_HEADS = 8
N_KV_HEADS = 2
Q_PER_KV = N_Q_HEADS // N_KV_HEADS
N_X_HEADS = 4
WINDOW = 128
ROPE_THETA = 500000.0
ROT_DIM = 16
ROT_HALF = ROT_DIM // 2
POOL_GROUP = 64
POOL_HALO = 16
PAST_LEN = 8192
LN_EPS = 1e-5
SM_SCALE = HEAD_DIM ** -0.5

VMEM_LIMIT_BYTES_V7X = 56 * 1024 * 1024
LANES = 128

TOKEN_TILE = 512
FF_CHUNK = 256
SAMPLE_SEQS = 8


def _cparams(*semantics):
    return pltpu.CompilerParams(dimension_semantics=semantics, vmem_limit_bytes=VMEM_LIMIT_BYTES_V7X)


def _resident(shape, index_map):
    return pl.BlockSpec(shape, index_map, pipeline_mode=pl.Buffered(1))


def _dot(a, b):
    return jnp.dot(a, b, preferred_element_type=F32)


def _dot_nt(a, b):
    return lax.dot_general(a, b, (((1,), (1,)), ((), ())), preferred_element_type=F32)


def _bdot_nt(a, b):
    return lax.dot_general(a, b, (((2,), (2,)), ((0,), (0,))), preferred_element_type=F32)


def _bdot(a, b):
    return lax.dot_general(a, b, (((2,), (1,)), ((0,), (0,))), preferred_element_type=F32)


def _layer_norm(z, g, b):
    mu = jnp.mean(z, axis=-1, keepdims=True)
    zc = z - mu
    var = jnp.mean(zc * zc, axis=-1, keepdims=True)
    return zc * lax.rsqrt(var + LN_EPS) * g + b


def _split_rows(n_first_tiles):
    first = lambda i: (jnp.minimum(i, n_first_tiles - 1), 0)
    second = lambda i: (jnp.maximum(i - n_first_tiles, 0), 0)
    return first, second


def _rope_table_kernel(cos_ref, sp_ref, sm_ref, *, seq, dec_seq):
    n = cos_ref.shape[0]
    r = lax.broadcasted_iota(jnp.int32, (n, LANES), 0)
    lane = lax.broadcasted_iota(jnp.int32, (n, LANES), 1)
    pos = jnp.where(r < seq, r, PAST_LEN + (r - seq) % dec_seq)
    d = lane % HEAD_DIM
    fidx = (d % ROT_HALF).astype(F32)
    inv_freq = jnp.power(ROPE_THETA, -fidx * (2.0 / ROT_DIM))
    ang = pos.astype(F32) * inv_freq
    cos = jnp.cos(ang)
    sin = jnp.sin(ang)
    cos_ref[...] = jnp.where(d < ROT_DIM, cos, 1.0)
    sp_ref[...] = jnp.where((d >= ROT_HALF) & (d < ROT_DIM), sin, 0.0)
    sm_ref[...] = jnp.where(d < ROT_HALF, -sin, 0.0)


def _rope_tables(seq, n_sample_rows, dec_seq):
    n = seq + n_sample_rows
    out = jax.ShapeDtypeStruct((n, LANES), F32)
    return pl.pallas_call(
        functools.partial(_rope_table_kernel, seq=seq, dec_seq=dec_seq),
        out_shape=(out, out, out),
        name="rope_tables",
    )()


def _mem_kv_kernel(m_ref, wk_ref, wv_ref, k_ref, v_ref):
    m = m_ref[...]
    k_ref[...] = _dot(m, wk_ref[...])
    v_ref[...] = _dot(m, wv_ref[...])


def _mem_kv(mem, w_k, w_v):
    depth, d, xw = w_k.shape
    rows = mem.shape[0]
    out = jax.ShapeDtypeStruct((depth, rows, xw), F32)
    w_spec = pl.BlockSpec((None, d, xw), lambda l: (l, 0, 0))
    o_spec = pl.BlockSpec((None, rows, xw), lambda l: (l, 0, 0))
    return pl.pallas_call(
        _mem_kv_kernel,
        grid=(depth,),
        in_specs=[pl.BlockSpec((rows, d), lambda l: (0, 0)), w_spec, w_spec],
        out_specs=(o_spec, o_spec),
        out_shape=(out, out),
        compiler_params=_cparams("parallel"),
        name="mem_kv",
    )(mem, w_k, w_v)


def _ffn_kernel(*refs, alpha, n_first_tiles, split_in, split_out, proj_widths):
    refs = list(refs)
    x_refs = [refs.pop(0) for _ in range(2 if split_in else 1)]
    wg_ref, wu_ref, wd_ref, g_ref, b_ref = (refs.pop(0) for _ in range(5))
    if proj_widths:
        wp_ref, cos_ref, sp_ref, sm_ref = (refs.pop(0) for _ in range(4))
    o_refs = [refs.pop(0) for _ in range(2 if split_out else 1)]
    if proj_widths:
        xp_ref, q_ref, k_ref, v_ref, xq_ref = (refs.pop(0) for _ in range(5))
    (a_ref,) = refs
    i = pl.program_id(0)

    x = x_refs[0][...]
    if split_in:
        x = jnp.where(i < n_first_tiles, x, x_refs[1][...])
    xb = x.astype(BF16)
    d_ff = wg_ref.shape[1]
    for c in range(0, d_ff, FF_CHUNK):
        g = _dot(xb, wg_ref[:, c:c + FF_CHUNK])
        u = _dot(xb, wu_ref[:, c:c + FF_CHUNK])
        a_ref[:, c:c + FF_CHUNK] = (g * jax.nn.sigmoid(g) * u).astype(BF16)
    y = _dot(a_ref[...], wd_ref[...])
    out = _layer_norm(alpha * x + 0.5 * y, g_ref[...], b_ref[...])

    if split_out:
        @pl.when(i < n_first_tiles)
        def _():
            o_refs[0][...] = out
        o_refs[1][...] = out
    else:
        o_refs[0][...] = out

    if proj_widths:
        pool_w, swa_w, kv_w, x_w = proj_widths
        ob = out.astype(BF16)
        cos, sp, sm = cos_ref[...], sp_ref[...], sm_ref[...]

        def rope(t):
            return t * cos + pltpu.roll(t, ROT_HALF, 1) * sp + pltpu.roll(t, LANES - ROT_HALF, 1) * sm

        c0 = 0
        xp_ref[...] = _dot(ob, wp_ref[:, c0:c0 + pool_w])
        c0 += pool_w
        q = _dot(ob, wp_ref[:, c0:c0 + swa_w])
        for s in range(swa_w // LANES):
            q_ref[:, s * LANES:(s + 1) * LANES] = rope(q[:, s * LANES:(s + 1) * LANES]) * SM_SCALE
        c0 += swa_w
        kv = _dot(ob, wp_ref[:, c0:c0 + 2 * kv_w])
        k_ref[...] = rope(kv[:, 0:kv_w])
        v_ref[...] = kv[:, kv_w:]
        c0 += 2 * kv_w
        xq_ref[...] = _dot(ob, wp_ref[:, c0:c0 + x_w]) * SM_SCALE


def _ffn(xs, w_gate, w_up, w_down, ln_g, ln_b, layer, alpha, n_prompt, split_out=False, proj=None):
    split_in = len(xs) == 2
    d = xs[0].shape[1]
    n = sum(x.shape[0] for x in xs)
    d_ff = w_gate.shape[2]
    tm = TOKEN_TILE
    n_first_tiles = n_prompt // tm
    first, second = _split_rows(n_first_tiles)
    whole = lambda i: (i, 0)
    lsel = lambda i: (layer, 0, 0)

    in_specs = [pl.BlockSpec((tm, d), first), pl.BlockSpec((tm, d), second)] if split_in else [pl.BlockSpec((tm, d), whole)]
    in_specs += [
        _resident((None, d, d_ff), lsel), _resident((None, d, d_ff), lsel), _resident((None, d_ff, d), lsel),
        _resident((None, 1, d), lsel), _resident((None, 1, d), lsel),
    ]
    args = list(xs) + [w_gate, w_up, w_down, ln_g, ln_b]
    if split_out:
        out_specs = [pl.BlockSpec((tm, d), first), pl.BlockSpec((tm, d), second)]
        out_shape = [jax.ShapeDtypeStruct((n_prompt, d), F32), jax.ShapeDtypeStruct((n - n_prompt, d), F32)]
    else:
        out_specs = [pl.BlockSpec((tm, d), whole)]
        out_shape = [jax.ShapeDtypeStruct((n, d), F32)]
    widths = None
    if proj is not None:
        w_in, tables, seq, widths = proj
        pool_w, swa_w, kv_w, x_w = widths
        assert kv_w == LANES
        seq_tiles = seq // tm

        def table_idx(i):
            return (jnp.where(i < n_first_tiles, i % seq_tiles, seq_tiles + i - n_first_tiles), 0)

        t_spec = pl.BlockSpec((tm, LANES), table_idx)
        in_specs += [_resident((None, d, sum(widths) + kv_w), lsel), t_spec, t_spec, t_spec]
        args += [w_in, *tables]
        for w in (pool_w, swa_w, kv_w, kv_w, x_w):
            out_specs.append(pl.BlockSpec((tm, w), whole))
            out_shape.append(jax.ShapeDtypeStruct((n, w), F32))
    return pl.pallas_call(
        functools.partial(_ffn_kernel, alpha=alpha, n_first_tiles=n_first_tiles, split_in=split_in,
                          split_out=split_out, proj_widths=widths),
        grid=(n // tm,),
        in_specs=in_specs,
        out_specs=out_specs,
        out_shape=out_shape,
        scratch_shapes=[pltpu.VMEM((tm, d_ff), BF16)],
        compiler_params=_cparams("arbitrary"),
        name="ffn_proj" if proj is not None else "ffn",
    )(*args)


def _pool_window_sums(e_ref, w2_ref, w4_ref, w8_ref, n):
    h = 2 * POOL_HALO
    lead = (slice(None),) * (len(e_ref.shape) - 2)
    rows = lambda a, b: lead + (slice(a, b), slice(None))
    sl = lambda ref, a, b: ref[rows(a, b)]
    w2_ref[rows(8, n)] = sl(e_ref, 8, n) + sl(e_ref, 7, n - 1)
    w4_ref[rows(16, n)] = sl(w2_ref, 16, n) + sl(w2_ref, 14, n - 2)
    w8_ref[rows(24, n)] = sl(w4_ref, 24, n) + sl(w4_ref, 20, n - 4)
    w16 = sl(w8_ref, h, n) + sl(w8_ref, h - 8, n - 8)
    return sl(w2_ref, h, n), sl(w4_ref, h, n), sl(w8_ref, h, n), w16


def _pool_delta(sums, xp, pos):
    w2, w4, w8, w16 = sums
    grp = lax.broadcasted_iota(jnp.int32, xp.shape, xp.ndim - 1) // POOL_GROUP
    s = jnp.where(grp == 0, w2, jnp.where(grp == 1, w4, jnp.where(grp == 2, w8, w16)))
    width = jnp.left_shift(2, grp)
    cnt = jnp.minimum(width, pos + 1).astype(F32)
    return s / cnt - xp


def _head_column(values, rows_per_head):
    n = len(values) * rows_per_head
    head = lax.broadcasted_iota(jnp.int32, (n, 1), 0) // rows_per_head
    col = values[-1]
    for i in reversed(range(len(values) - 1)):
        col = jnp.where(head <= i, values[i], col)
    return col


def _softmax(s, sink=None):
    m = jnp.max(s, axis=-1, keepdims=True)
    if sink is not None:
        m = jnp.maximum(m, sink)
    p = jnp.exp(s - m)
    den = jnp.sum(p, axis=-1, keepdims=True)
    if sink is not None:
        den = den + jnp.exp(sink - m)
    return p * (1.0 / den)


def _mix_prompt_kernel(sink_ref, xp_ref, xph_ref, q_ref, k_ref, kh_ref, v_ref, vh_ref, xq_ref,
                       mk_ref, mv_ref, pbd_ref, pscale_ref, y_ref,
                       e_ref, w2_ref, w4_ref, w8_ref, kb_ref, vb_ref, *, layer):
    t = pl.program_id(1)
    tq = xp_ref.shape[0]
    pool_w = xp_ref.shape[1]
    swa_w = q_ref.shape[1]

    xp = xp_ref[...]
    e_ref[0:POOL_HALO, :] = jnp.zeros((POOL_HALO, pool_w), F32)
    e_ref[POOL_HALO:2 * POOL_HALO, :] = jnp.where(t == 0, 0.0, xph_ref[...])
    e_ref[2 * POOL_HALO:, :] = xp
    sums = _pool_window_sums(e_ref, w2_ref, w4_ref, w8_ref, tq + 2 * POOL_HALO)
    pos = t * tq + lax.broadcasted_iota(jnp.int32, (tq, pool_w), 0)
    delta = _pool_delta(sums, xp, pos)
    y_pool = _dot(delta.astype(BF16), pbd_ref[...]) * pscale_ref[...]
    y_ref[:, 0:pool_w] = y_pool.astype(BF16)

    for g in range(N_KV_HEADS):
        cs = slice(g * HEAD_DIM, (g + 1) * HEAD_DIM)
        kb_ref[g, 0:WINDOW, :] = kh_ref[:, cs].astype(BF16)
        kb_ref[g, WINDOW:, :] = k_ref[:, cs].astype(BF16)
        vb_ref[g, 0:WINDOW, :] = vh_ref[:, cs].astype(BF16)
        vb_ref[g, WINDOW:, :] = v_ref[:, cs].astype(BF16)
    n_rows = Q_PER_KV * WINDOW
    qi = lax.broadcasted_iota(jnp.int32, (n_rows, WINDOW), 0) % WINDOW
    from_prev = lax.broadcasted_iota(jnp.int32, (n_rows, WINDOW), 1) > qi
    for j in range(tq // WINDOW):
        rows = slice(j * WINDOW, (j + 1) * WINDOW)
        keys = slice(j * WINDOW, (j + 2) * WINDOW)
        outs = []
        for g in range(N_KV_HEADS):
            heads = range(g * Q_PER_KV, (g + 1) * Q_PER_KV)
            qg = jnp.concatenate([q_ref[rows, h * HEAD_DIM:(h + 1) * HEAD_DIM] for h in heads], axis=0)
            s = _dot_nt(qg.astype(BF16), kb_ref[g, keys, :])
            s_prev = s[:, :WINDOW]
            if j == 0:
                s_prev = jnp.where(t > 0, s_prev, -jnp.inf)
            sink = _head_column([sink_ref[layer, h] for h in heads], WINDOW)
            p = _softmax(jnp.where(from_prev, s_prev, s[:, WINDOW:]), sink)
            p2 = jnp.concatenate([jnp.where(from_prev, p, 0.0), jnp.where(from_prev, 0.0, p)], axis=1)
            o = _dot(p2.astype(BF16), vb_ref[g, keys, :])
            outs.extend(o[i * WINDOW:(i + 1) * WINDOW, :] for i in range(Q_PER_KV))
        y_ref[rows, pool_w:pool_w + swa_w] = jnp.concatenate(outs, axis=1).astype(BF16)

    outs = []
    for h in range(N_X_HEADS):
        cs = slice(h * HEAD_DIM, (h + 1) * HEAD_DIM)
        p = _softmax(_dot_nt(xq_ref[:, cs].astype(BF16), mk_ref[:, cs].astype(BF16)))
        outs.append(_dot(p.astype(BF16), mv_ref[:, cs].astype(BF16)))
    y_ref[:, pool_w + swa_w:] = jnp.concatenate(outs, axis=1).astype(BF16)


def _mix_prompt(sink, xp, q, k, v, xq, mk, mv, pool_bd, pool_scale, layer, batch, seq):
    tq = TOKEN_TILE
    n_t = seq // tq
    pool_w, swa_w, kv_w, x_w = xp.shape[1], q.shape[1], k.shape[1], xq.shape[1]
    n_mem = mk.shape[1] // batch
    d_out = pool_w + swa_w + x_w

    tile = lambda w: pl.BlockSpec((tq, w), lambda b, t: (b * n_t + t, 0))

    def halo(rows, w):
        per_tile = tq // rows
        return pl.BlockSpec((rows, w), lambda b, t: (jnp.maximum((b * n_t + t) * per_tile - 1, 0), 0))

    mem = pl.BlockSpec((None, n_mem, x_w), lambda b, t: (layer, b, 0))
    return pl.pallas_call(
        functools.partial(_mix_prompt_kernel, layer=layer),
        grid=(batch, n_t),
        in_specs=[
            pl.BlockSpec(memory_space=pltpu.SMEM),
            tile(pool_w), halo(POOL_HALO, pool_w),
            tile(swa_w),
            tile(kv_w), halo(WINDOW, kv_w),
            tile(kv_w), halo(WINDOW, kv_w),
            tile(x_w),
            mem, mem,
            _resident((None, pool_w, pool_w), lambda b, t: (layer, 0, 0)),
            _resident((None, 1, pool_w), lambda b, t: (layer, 0, 0)),
        ],
        out_specs=pl.BlockSpec((tq, d_out), lambda b, t: (b * n_t + t, 0)),
        out_shape=jax.ShapeDtypeStruct((batch * seq, d_out), BF16),
        scratch_shapes=[
            pltpu.VMEM((tq + 2 * POOL_HALO, pool_w), F32),
            pltpu.VMEM((tq + 2 * POOL_HALO, pool_w), F32),
            pltpu.VMEM((tq + 2 * POOL_HALO, pool_w), F32),
            pltpu.VMEM((tq + 2 * POOL_HALO, pool_w), F32),
            pltpu.VMEM((N_KV_HEADS, tq + WINDOW, HEAD_DIM), BF16),
            pltpu.VMEM((N_KV_HEADS, tq + WINDOW, HEAD_DIM), BF16),
        ],
        compiler_params=_cparams("parallel", "parallel"),
        name="mix_prompt",
    )(sink, xp, xp, q, k, k, v, v, xq, mk, mv, pool_bd, pool_scale)


def _mix_sample_kernel(sink_ref, xp_ref, q_ref, k_ref, v_ref, xq_ref, ck_ref, cv_ref, st_ref,
                       cmk_ref, cmv_ref, pbd_ref, pscale_ref,
                       y_ref, nk_ref, nv_ref, npool_ref,
                       e_ref, w2_ref, w4_ref, w8_ref, kb_ref, vb_ref, *, layer, dec_seq):
    sb = ck_ref.shape[0]
    pool_w = xp_ref.shape[1]
    swa_w = q_ref.shape[1]
    kv_w = k_ref.shape[1]
    x_w = xq_ref.shape[1]
    n_keys = kb_ref.shape[2]
    h2 = 2 * POOL_HALO
    per_seq = lambda a: a.reshape(sb, dec_seq, a.shape[-1])
    flat = lambda a: a.reshape(sb * dec_seq, a.shape[-1])

    xp = per_seq(xp_ref[...])
    e_ref[:, 0:POOL_HALO, :] = jnp.zeros((sb, POOL_HALO, pool_w), F32)
    e_ref[:, POOL_HALO:h2, :] = st_ref[...]
    e_ref[:, h2:, :] = xp
    sums = _pool_window_sums(e_ref, w2_ref, w4_ref, w8_ref, h2 + dec_seq)
    pos = PAST_LEN + lax.broadcasted_iota(jnp.int32, xp.shape, 1)
    delta = flat(_pool_delta(sums, xp, pos))
    y_pool = _dot(delta.astype(BF16), pbd_ref[...]) * pscale_ref[...]
    y_ref[:, 0:pool_w] = y_pool.astype(BF16)
    npool_ref[...] = e_ref[:, h2 + dec_seq - POOL_HALO:, :]

    k_new = per_seq(k_ref[...])
    v_new = per_seq(v_ref[...])
    nk_ref[:, 0:WINDOW - dec_seq, :] = ck_ref[:, dec_seq:, :]
    nk_ref[:, WINDOW - dec_seq:, :] = k_new
    nv_ref[:, 0:WINDOW - dec_seq, :] = cv_ref[:, dec_seq:, :]
    nv_ref[:, WINDOW - dec_seq:, :] = v_new

    pad = jnp.zeros((sb, dec_seq, HEAD_DIM), F32)
    tail = jnp.zeros((sb, n_keys - WINDOW - 2 * dec_seq, HEAD_DIM), BF16)
    for g in range(N_KV_HEADS):
        cs = slice(g * HEAD_DIM, (g + 1) * HEAD_DIM)
        for buf_ref, cache_ref, new in ((kb_ref, ck_ref, k_new), (vb_ref, cv_ref, v_new)):
            buf_ref[g, :, 0:WINDOW, :] = cache_ref[:, :, cs].astype(BF16)
            buf_ref[g, :, WINDOW:WINDOW + 2 * dec_seq, :] = jnp.concatenate([new[:, :, cs], pad], axis=1).astype(BF16)
            buf_ref[g, :, WINDOW + 2 * dec_seq:, :] = tail
    n_rows = Q_PER_KV * dec_seq
    qi = lax.broadcasted_iota(jnp.int32, (n_rows, n_keys), 0) % dec_seq
    kj = lax.broadcasted_iota(jnp.int32, (n_rows, n_keys), 1)
    band = ((kj > qi) & (kj <= qi + WINDOW))[None]
    q = per_seq(q_ref[...])
    outs = []
    for g in range(N_KV_HEADS):
        heads = range(g * Q_PER_KV, (g + 1) * Q_PER_KV)
        qg = jnp.concatenate([q[:, :, h * HEAD_DIM:(h + 1) * HEAD_DIM] for h in heads], axis=1)
        s = _bdot_nt(qg.astype(BF16), kb_ref[g])
        sink = _head_column([sink_ref[layer, h] for h in heads], dec_seq)[None]
        p = _softmax(jnp.where(band, s, -jnp.inf), sink)
        o = _bdot(p.astype(BF16), vb_ref[g])
        outs.extend(o[:, i * dec_seq:(i + 1) * dec_seq, :] for i in range(Q_PER_KV))
    y_ref[:, pool_w:pool_w + swa_w] = flat(jnp.concatenate(outs, axis=2)).astype(BF16)

    xq = per_seq(xq_ref[...])
    outs = []
    for h in range(N_X_HEADS):
        cs = slice(h * HEAD_DIM, (h + 1) * HEAD_DIM)
        p = _softmax(_bdot_nt(xq[:, :, cs].astype(BF16), cmk_ref[:, :, cs].astype(BF16)))
        outs.append(_bdot(p.astype(BF16), cmv_ref[:, :, cs].astype(BF16)))
    y_ref[:, pool_w + swa_w:] = flat(jnp.concatenate(outs, axis=2)).astype(BF16)


def _mix_sample(sink, xp, q, k, v, xq, cache_k, cache_v, state16, cache_mk, cache_mv, pool_bd, pool_scale,
                layer, n_prompt, dec_seq):
    dec_batch = cache_k.shape[1]
    sb = SAMPLE_SEQS
    rows = sb * dec_seq
    off = n_prompt // rows
    pool_w, swa_w, kv_w, x_w = xp.shape[1], q.shape[1], k.shape[1], xq.shape[1]
    n_mem = cache_mk.shape[2]
    d_out = pool_w + swa_w + x_w
    n_keys = 2 * WINDOW

    tok = lambda w: pl.BlockSpec((rows, w), lambda i: (off + i, 0))
    per_seq = lambda a, w: pl.BlockSpec((None, sb, a, w), lambda i: (layer, i, 0, 0))
    out_seq = lambda a, w: pl.BlockSpec((sb, a, w), lambda i: (i, 0, 0))
    e_shape = (sb, 2 * POOL_HALO + dec_seq, pool_w)
    return pl.pallas_call(
        functools.partial(_mix_sample_kernel, layer=layer, dec_seq=dec_seq),
        grid=(dec_batch // sb,),
        in_specs=[
            pl.BlockSpec(memory_space=pltpu.SMEM),
            tok(pool_w), tok(swa_w), tok(kv_w), tok(kv_w), tok(x_w),
            per_seq(WINDOW, kv_w), per_seq(WINDOW, kv_w), per_seq(POOL_HALO, pool_w),
            per_seq(n_mem, x_w), per_seq(n_mem, x_w),
            _resident((None, pool_w, pool_w), lambda i: (layer, 0, 0)),
            _resident((None, 1, pool_w), lambda i: (layer, 0, 0)),
        ],
        out_specs=(
            pl.BlockSpec((rows, d_out), lambda i: (i, 0)),
            out_seq(WINDOW, kv_w), out_seq(WINDOW, kv_w), out_seq(POOL_HALO, pool_w),
        ),
        out_shape=(
            jax.ShapeDtypeStruct((dec_batch * dec_seq, d_out), BF16),
            jax.ShapeDtypeStruct((dec_batch, WINDOW, kv_w), F32),
            jax.ShapeDtypeStruct((dec_batch, WINDOW, kv_w), F32),
            jax.ShapeDtypeStruct((dec_batch, POOL_HALO, pool_w), F32),
        ),
        scratch_shapes=[
            pltpu.VMEM(e_shape, F32), pltpu.VMEM(e_shape, F32), pltpu.VMEM(e_shape, F32), pltpu.VMEM(e_shape, F32),
            pltpu.VMEM((N_KV_HEADS, sb, n_keys, HEAD_DIM), BF16),
            pltpu.VMEM((N_KV_HEADS, sb, n_keys, HEAD_DIM), BF16),
        ],
        compiler_params=_cparams("parallel"),
        name="mix_sample",
    )(sink, xp, q, k, v, xq, cache_k, cache_v, state16, cache_mk, cache_mv, pool_bd, pool_scale)


def _merge_ln_kernel(x_ref, yp_ref, ys_ref, win_ref, wbp_ref, wbs_ref, wbx_ref, wo_ref, g_ref, b_ref, o_ref,
                     *, alpha, n_first_tiles, gate_col0):
    x = x_ref[...]
    xb = x.astype(BF16)
    d = x.shape[1]
    y = jnp.where(pl.program_id(0) < n_first_tiles, yp_ref[...], ys_ref[...])
    c0 = 0
    merged = None
    for b, w_ref in enumerate((wbp_ref, wbs_ref, wbx_ref)):
        w = w_ref.shape[0]
        gate = jax.nn.sigmoid(_dot(xb, win_ref[:, gate_col0 + b * d:gate_col0 + (b + 1) * d]))
        term = gate * _dot(y[:, c0:c0 + w], w_ref[...])
        merged = term if merged is None else merged + term
        c0 += w
    z = alpha * x + _dot(merged.astype(BF16), wo_ref[...])
    o_ref[...] = _layer_norm(z, g_ref[...], b_ref[...])


def _merge_ln(x, y_p, y_s, w_in, w_br_pool, w_br_swa, w_br_cross, w_o, ln_g, ln_b, layer, alpha, gate_col0):
    n, d = x.shape
    tm = TOKEN_TILE
    n_first_tiles = y_p.shape[0] // tm
    first, second = _split_rows(n_first_tiles)
    lsel = lambda i: (layer, 0, 0)
    res = lambda w: _resident((None,) + w.shape[1:], lsel)
    return pl.pallas_call(
        functools.partial(_merge_ln_kernel, alpha=alpha, n_first_tiles=n_first_tiles, gate_col0=gate_col0),
        grid=(n // tm,),
        in_specs=[
            pl.BlockSpec((tm, d), lambda i: (i, 0)),
            pl.BlockSpec((tm, y_p.shape[1]), first),
            pl.BlockSpec((tm, y_s.shape[1]), second),
            res(w_in), res(w_br_pool), res(w_br_swa), res(w_br_cross), res(w_o), res(ln_g), res(ln_b),
        ],
        out_specs=pl.BlockSpec((tm, d), lambda i: (i, 0)),
        out_shape=jax.ShapeDtypeStruct((n, d), F32),
        compiler_params=_cparams("arbitrary"),
        name="merge_ln",
    )(x, y_p, y_s, w_in, w_br_pool, w_br_swa, w_br_cross, w_o, ln_g, ln_b)


def kernel(x_prompt, x_sample, cache_swa_k, cache_swa_v, state_pool, cache_mem_k, cache_mem_v, mem_prompt, w_mem_k, w_mem_v, ffn1_w_gate, ffn1_w_up, ffn1_w_down, ln1_g, ln1_b, w_in, pool_mix, pool_scale, attn_sink, w_br_pool, w_br_swa, w_br_cross, w_o, ln2_g, ln2_b, ffn2_w_gate, ffn2_w_up, ffn2_w_down, ln3_g, ln3_b):
    batch, seq, d = x_prompt.shape
    dec_batch, dec_seq, _ = x_sample.shape
    depth = w_in.shape[0]
    n_groups, pool_group = pool_mix.shape[1], pool_mix.shape[2]
    pool_w = n_groups * pool_group
    swa_w = w_br_swa.shape[1]
    kv_w = cache_swa_k.shape[3] * cache_swa_k.shape[4]
    x_w = w_br_cross.shape[1]
    n_mem = mem_prompt.shape[1]
    widths = (pool_w, swa_w, kv_w, x_w)
    proj_w = sum(widths) + kv_w
    assert pool_group == POOL_GROUP and swa_w == N_Q_HEADS * HEAD_DIM and kv_w == N_KV_HEADS * HEAD_DIM
    assert x_w == N_X_HEADS * HEAD_DIM and cache_swa_k.shape[2] == WINDOW
    assert state_pool.shape[2] == POOL_HALO - 1 and w_in.shape[2] == proj_w + 3 * d
    alpha = (2 * depth) ** 0.25
    n_prompt, n_sample = batch * seq, dec_batch * dec_seq

    bf = lambda w: w.astype(BF16)
    vec = lambda p: p.reshape(depth, 1, p.shape[-1])
    w_in_b = bf(w_in)
    pool_bd = bf(jnp.einsum("lgcd,gh->lgchd", pool_mix, jnp.eye(n_groups, dtype=F32))
                 .reshape(depth, pool_w, pool_w))
    state16 = jnp.pad(state_pool, ((0, 0), (0, 0), (1, 0), (0, 0)))
    cache_k = cache_swa_k.reshape(depth, dec_batch, WINDOW, kv_w)
    cache_v = cache_swa_v.reshape(depth, dec_batch, WINDOW, kv_w)
    cache_mk = cache_mem_k.reshape(depth, dec_batch, n_mem, x_w)
    cache_mv = cache_mem_v.reshape(depth, dec_batch, n_mem, x_w)

    tables = _rope_tables(seq, n_sample, dec_seq)
    mk, mv = _mem_kv(bf(mem_prompt.reshape(batch * n_mem, d)), bf(w_mem_k), bf(w_mem_v))

    ffn1 = (bf(ffn1_w_gate), bf(ffn1_w_up), bf(ffn1_w_down), vec(ln1_g), vec(ln1_b))
    ffn2 = (bf(ffn2_w_gate), bf(ffn2_w_up), bf(ffn2_w_down), vec(ln3_g), vec(ln3_b))
    merge = (w_in_b, bf(w_br_pool), bf(w_br_swa), bf(w_br_cross), bf(w_o), vec(ln2_g), vec(ln2_b))
    pscale = vec(pool_scale)

    xs = (x_prompt.reshape(n_prompt, d), x_sample.reshape(n_sample, d))
    pk, pv, pp, sk, sv, sp = [], [], [], [], [], []
    for l in range(depth):
        x, xp, q, k, v, xq = _ffn(xs, *ffn1, l, alpha, n_prompt, proj=(w_in_b, tables, seq, widths))
        y_p = _mix_prompt(attn_sink, xp, q, k, v, xq, mk, mv, pool_bd, pscale, l, batch, seq)
        y_s, nk, nv, npool = _mix_sample(attn_sink, xp, q, k, v, xq, cache_k, cache_v, state16,
                                         cache_mk, cache_mv, pool_bd, pscale, l, n_prompt, dec_seq)
        x = _merge_ln(x, y_p, y_s, *merge, l, alpha, proj_w)
        xs = _ffn((x,), *ffn2, l, alpha, n_prompt, split_out=(l == depth - 1))
        last = lambda a, rows: a[:n_prompt].reshape(batch, seq, a.shape[1])[:, seq - rows:]
        pk.append(last(k, WINDOW))
        pv.append(last(v, WINDOW))
        pp.append(last(xp, POOL_HALO - 1))
        sk.append(nk)
        sv.append(nv)
        sp.append(npool[:, 1:])

    heads = lambda a, n_h: a.reshape(a.shape[:-1] + (n_h, HEAD_DIM))
    return (xs[0].reshape(batch, seq, d), xs[1].reshape(dec_batch, dec_seq, d),
            heads(jnp.stack(pk), N_KV_HEADS), heads(jnp.stack(pv), N_KV_HEADS), jnp.stack(pp),
            heads(mk.reshape(depth, batch, n_mem, x_w), N_X_HEADS),
            heads(mv.reshape(depth, batch, n_mem, x_w), N_X_HEADS),
            heads(jnp.stack(sk), N_KV_HEADS), heads(jnp.stack(sv), N_KV_HEADS), jnp.stack(sp))
```

```python
import functools

import jax
import jax.numpy as jnp
from jax import lax
from jax.experimental import pallas as pl
from jax.experimental.pallas import tpu as pltpu

F32 = jnp.float32
BF16 = jnp.bfloat16

HEAD_DIM = 64
N_Q_HEADS = 8
N_KV_HEADS = 2
Q_PER_KV = N_Q_HEADS // N_KV_HEADS
N_X_HEADS = 4
WINDOW = 128
ROPE_THETA = 500000.0
ROT_DIM = 16
ROT_HALF = ROT_DIM // 2
POOL_GROUP = 64
POOL_HALO = 16
PAST_LEN = 8192
LN_EPS = 1e-5
SM_SCALE = HEAD_DIM ** -0.5

VMEM_LIMIT_BYTES_V7X = 56 * 1024 * 1024
LANES = 128

TOKEN_TILE = 512
WIDE_TOKEN_TILE = 1024
SUB_TILE = 512
FF_CHUNK = 256
SAMPLE_SEQS = 8


def _cparams(*semantics):
    return pltpu.CompilerParams(dimension_semantics=semantics, vmem_limit_bytes=VMEM_LIMIT_BYTES_V7X)


def _resident(shape, index_map):
    return pl.BlockSpec(shape, index_map, pipeline_mode=pl.Buffered(1))


def _dot(a, b):
    return jnp.dot(a, b, preferred_element_type=F32)


def _dot_nt(a, b):
    return lax.dot_general(a, b, (((1,), (1,)), ((), ())), preferred_element_type=F32)


def _bdot_nt(a, b):
    return lax.dot_general(a, b, (((2,), (2,)), ((0,), (0,))), preferred_element_type=F32)


def _bdot(a, b):
    return lax.dot_general(a, b, (((2,), (1,)), ((0,), (0,))), preferred_element_type=F32)


def _layer_norm(z, g, b):
    mu = jnp.mean(z, axis=-1, keepdims=True)
    zc = z - mu
    var = jnp.mean(zc * zc, axis=-1, keepdims=True)
    return zc * lax.rsqrt(var + LN_EPS) * g + b


def _split_rows(n_first_tiles):
    first = lambda i: (jnp.minimum(i, n_first_tiles - 1), 0)
    second = lambda i: (jnp.maximum(i - n_first_tiles, 0), 0)
    return first, second


def _rope_table_kernel(cos_ref, sp_ref, sm_ref, *, seq, dec_seq):
    n = cos_ref.shape[0]
    r = lax.broadcasted_iota(jnp.int32, (n, LANES), 0)
    lane = lax.broadcasted_iota(jnp.int32, (n, LANES), 1)
    pos = jnp.where(r < seq, r, PAST_LEN + (r - seq) % dec_seq)
    d = lane % HEAD_DIM
    fidx = (d % ROT_HALF).astype(F32)
    inv_freq = jnp.power(ROPE_THETA, -fidx * (2.0 / ROT_DIM))
    ang = pos.astype(F32) * inv_freq
    cos = jnp.cos(ang)
    sin = jnp.sin(ang)
    cos_ref[...] = jnp.where(d < ROT_DIM, cos, 1.0)
    sp_ref[...] = jnp.where((d >= ROT_HALF) & (d < ROT_DIM), sin, 0.0)
    sm_ref[...] = jnp.where(d < ROT_HALF, -sin, 0.0)


def _rope_tables(seq, n_sample_rows, dec_seq):
    n = seq + n_sample_rows
    out = jax.ShapeDtypeStruct((n, LANES), F32)
    return pl.pallas_call(
        functools.partial(_rope_table_kernel, seq=seq, dec_seq=dec_seq),
        out_shape=(out, out, out),
        name="rope_tables",
    )()


def _mem_kv_kernel(m_ref, wk_ref, wv_ref, k_ref, v_ref):
    m = m_ref[...]
    k_ref[...] = _dot(m, wk_ref[...])
    v_ref[...] = _dot(m, wv_ref[...])


def _mem_kv(mem, w_k, w_v):
    depth, d, xw = w_k.shape
    rows = mem.shape[0]
    out = jax.ShapeDtypeStruct((depth, rows, xw), F32)
    w_spec = pl.BlockSpec((None, d, xw), lambda l: (l, 0, 0))
    o_spec = pl.BlockSpec((None, rows, xw), lambda l: (l, 0, 0))
    return pl.pallas_call(
        _mem_kv_kernel,
        grid=(depth,),
        in_specs=[pl.BlockSpec((rows, d), lambda l: (0, 0)), w_spec, w_spec],
        out_specs=(o_spec, o_spec),
        out_shape=(out, out),
        compiler_params=_cparams("parallel"),
        name="mem_kv",
    )(mem, w_k, w_v)


def _ffn_kernel(*refs, alpha, n_first_tiles, split_in, split_out, proj_widths):
    refs = list(refs)
    x_refs = [refs.pop(0) for _ in range(2 if split_in else 1)]
    wg_ref, wu_ref, wd_ref, g_ref, b_ref = (refs.pop(0) for _ in range(5))
    if proj_widths:
        wp_ref, cos_ref, sp_ref, sm_ref = (refs.pop(0) for _ in range(4))
    o_refs = [refs.pop(0) for _ in range(2 if split_out else 1)]
    if proj_widths:
        xp_ref, q_ref, k_ref, v_ref, xq_ref = (refs.pop(0) for _ in range(5))
    (a_ref,) = refs
    i = pl.program_id(0)
    d_ff = wg_ref.shape[1]

    def finish(rows, z):
        out = _layer_norm(z, g_ref[...], b_ref[...])

        if split_out:
            @pl.when(i < n_first_tiles)
            def _():
                o_refs[0][rows, :] = out
            o_refs[1][rows, :] = out
        else:
            o_refs[0][rows, :] = out

        if proj_widths:
            pool_w, swa_w, kv_w, x_w = proj_widths
            ob = out.astype(BF16)
            cos, sp, sm = cos_ref[rows, :], sp_ref[rows, :], sm_ref[rows, :]

            def rope(t):
                return t * cos + pltpu.roll(t, ROT_HALF, 1) * sp + pltpu.roll(t, LANES - ROT_HALF, 1) * sm

            c0 = 0
            xp_ref[rows, :] = _dot(ob, wp_ref[:, c0:c0 + pool_w])
            c0 += pool_w
            q = _dot(ob, wp_ref[:, c0:c0 + swa_w])
            for s in range(swa_w // LANES):
                slab = slice(s * LANES, (s + 1) * LANES)
                q_ref[rows, slab] = rope(q[:, slab]) * SM_SCALE
            c0 += swa_w
            kv = _dot(ob, wp_ref[:, c0:c0 + 2 * kv_w])
            k_ref[rows, :] = rope(kv[:, 0:kv_w])
            v_ref[rows, :] = kv[:, kv_w:]
            c0 += 2 * kv_w
            xq_ref[rows, :] = _dot(ob, wp_ref[:, c0:c0 + x_w]) * SM_SCALE

    pending = None
    for r0 in range(0, a_ref.shape[0], SUB_TILE):
        rows = slice(r0, r0 + SUB_TILE)
        x = x_refs[0][rows, :]
        if split_in:
            x = jnp.where(i < n_first_tiles, x, x_refs[1][rows, :])
        xb = x.astype(BF16)
        for c in range(0, d_ff, FF_CHUNK):
            g = _dot(xb, wg_ref[:, c:c + FF_CHUNK])
            u = _dot(xb, wu_ref[:, c:c + FF_CHUNK])
            a_ref[rows, c:c + FF_CHUNK] = (g * jax.nn.sigmoid(g) * u).astype(BF16)
            if c == FF_CHUNK and pending is not None:
                finish(*pending)
                pending = None
        y = _dot(a_ref[rows, :], wd_ref[...])
        pending = (rows, alpha * x + 0.5 * y)
    finish(*pending)


def _ffn(xs,w_gate, w_up, w_down, ln_g, ln_b, layer, alpha, n_prompt, split_out=False, proj=None):
    split_in = len(xs) == 2
    d = xs[0].shape[1]
    n = sum(x.shape[0] for x in xs)
    d_ff = w_gate.shape[2]
    tm = TOKEN_TILE if proj is not None else WIDE_TOKEN_TILE
    n_first_tiles = n_prompt // tm
    first, second = _split_rows(n_first_tiles)
    whole = lambda i: (i, 0)
    lsel = lambda i: (layer, 0, 0)

    in_specs = [pl.BlockSpec((tm, d), first), pl.BlockSpec((tm, d), second)] if split_in else [pl.BlockSpec((tm, d), whole)]
    in_specs += [
        _resident((None, d, d_ff), lsel), _resident((None, d, d_ff), lsel), _resident((None, d_ff, d), lsel),
        _resident((None, 1, d), lsel), _resident((None, 1, d), lsel),
    ]
    args = list(xs) + [w_gate, w_up, w_down, ln_g, ln_b]
    if split_out:
        out_specs = [pl.BlockSpec((tm, d), first), pl.BlockSpec((tm, d), second)]
        out_shape = [jax.ShapeDtypeStruct((n_prompt, d), F32), jax.ShapeDtypeStruct((n - n_prompt, d), F32)]
    else:
        out_specs = [pl.BlockSpec((tm, d), whole)]
        out_shape = [jax.ShapeDtypeStruct((n, d), F32)]
    widths = None
    if proj is not None:
        w_in, tables, seq, widths = proj
        pool_w, swa_w, kv_w, x_w = widths
        assert kv_w == LANES
        seq_tiles = seq // tm

        def table_idx(i):
            return (jnp.where(i < n_first_tiles, i % seq_tiles, seq_tiles + i - n_first_tiles), 0)

        t_spec = pl.BlockSpec((tm, LANES), table_idx)
        in_specs += [_resident((None, d, sum(widths) + kv_w), lsel), t_spec, t_spec, t_spec]
        args += [w_in, *tables]
        for w in (pool_w, swa_w, kv_w, kv_w, x_w):
            out_specs.append(pl.BlockSpec((tm, w), whole))
            out_shape.append(jax.ShapeDtypeStruct((n, w), F32))
    return pl.pallas_call(
        functools.partial(_ffn_kernel, alpha=alpha, n_first_tiles=n_first_tiles, split_in=split_in,
                          split_out=split_out, proj_widths=widths),
        grid=(n // tm,),
        in_specs=in_specs,
        out_specs=out_specs,
        out_shape=out_shape,
        scratch_shapes=[pltpu.VMEM((tm, d_ff), BF16)],
        compiler_params=_cparams("arbitrary"),
        name="ffn_proj" if proj is not None else "ffn",
    )(*args)


def _pool_window_sums(e_ref, w2_ref, w4_ref, w8_ref, n):
    h = 2 * POOL_HALO
    lead = (slice(None),) * (len(e_ref.shape) - 2)
    rows = lambda a, b: lead + (slice(a, b), slice(None))
    sl = lambda ref, a, b: ref[rows(a, b)]
    w2_ref[rows(8, n)] = sl(e_ref, 8, n) + sl(e_ref, 7, n - 1)
    w4_ref[rows(16, n)] = sl(w2_ref, 16, n) + sl(w2_ref, 14, n - 2)
    w8_ref[rows(24, n)] = sl(w4_ref, 24, n) + sl(w4_ref, 20, n - 4)
    w16 = sl(w8_ref, h, n) + sl(w8_ref, h - 8, n - 8)
    return sl(w2_ref, h, n), sl(w4_ref, h, n), sl(w8_ref, h, n), w16


def _pool_delta(sums, xp, pos):
    w2, w4, w8, w16 = sums
    grp = lax.broadcasted_iota(jnp.int32, xp.shape, xp.ndim - 1) // POOL_GROUP
    s = jnp.where(grp == 0, w2, jnp.where(grp == 1, w4, jnp.where(grp == 2, w8, w16)))
    width = jnp.left_shift(2, grp)
    cnt = jnp.minimum(width, pos + 1).astype(F32)
    return s / cnt - xp


def _head_column(values, rows_per_head):
    n = len(values) * rows_per_head
    head = lax.broadcasted_iota(jnp.int32, (n, 1), 0) // rows_per_head
    col = values[-1]
    for i in reversed(range(len(values) - 1)):
        col = jnp.where(head <= i, values[i], col)
    return col


def _softmax(s, sink=None):
    m = jnp.max(s, axis=-1, keepdims=True)
    if sink is not None:
        m = jnp.maximum(m, sink)
    p = jnp.exp(s - m)
    den = jnp.sum(p, axis=-1, keepdims=True)
    if sink is not None:
        den = den + jnp.exp(sink - m)
    return p * (1.0 / den)


def _mix_prompt_kernel(sink_ref, xp_ref, xph_ref, q_ref, k_ref, kh_ref, v_ref, vh_ref, xq_ref,
                       mk_ref, mv_ref, pbd_ref, pscale_ref, y_ref, nk_ref, nv_ref, npool_ref,
                       e_ref, w2_ref, w4_ref, w8_ref, kb_ref, vb_ref, *, layer):
    t = pl.program_id(1)
    tq = xp_ref.shape[0]
    pool_w = xp_ref.shape[1]
    swa_w = q_ref.shape[1]

    nk_ref[...] = k_ref[tq - WINDOW:, :]
    nv_ref[...] = v_ref[tq - WINDOW:, :]
    npool_ref[...] = xp_ref[tq - POOL_HALO:, :]

    xp = xp_ref[...]
    e_ref[0:POOL_HALO, :] = jnp.zeros((POOL_HALO, pool_w), F32)
    e_ref[POOL_HALO:2 * POOL_HALO, :] = jnp.where(t == 0, 0.0, xph_ref[...])
    e_ref[2 * POOL_HALO:, :] = xp
    sums = _pool_window_sums(e_ref, w2_ref, w4_ref, w8_ref, tq + 2 * POOL_HALO)
    pos = t * tq + lax.broadcasted_iota(jnp.int32, (tq, pool_w), 0)
    delta = _pool_delta(sums, xp, pos)
    y_pool = _dot(delta.astype(BF16), pbd_ref[...]) * pscale_ref[...]
    y_ref[:, 0:pool_w] = y_pool.astype(BF16)

    for g in range(N_KV_HEADS):
        cs = slice(g * HEAD_DIM, (g + 1) * HEAD_DIM)
        kb_ref[g, 0:WINDOW, :] = kh_ref[:, cs].astype(BF16)
        kb_ref[g, WINDOW:, :] = k_ref[:, cs].astype(BF16)
        vb_ref[g, 0:WINDOW, :] = vh_ref[:, cs].astype(BF16)
        vb_ref[g, WINDOW:, :] = v_ref[:, cs].astype(BF16)
    n_rows = Q_PER_KV * WINDOW
    qi = lax.broadcasted_iota(jnp.int32, (n_rows, WINDOW), 0) % WINDOW
    from_prev = lax.broadcasted_iota(jnp.int32, (n_rows, WINDOW), 1) > qi
    for j in range(tq // WINDOW):
        rows = slice(j * WINDOW, (j + 1) * WINDOW)
        keys = slice(j * WINDOW, (j + 2) * WINDOW)
        outs = []
        for g in range(N_KV_HEADS):
            heads = range(g * Q_PER_KV, (g + 1) * Q_PER_KV)
            qg = jnp.concatenate([q_ref[rows, h * HEAD_DIM:(h + 1) * HEAD_DIM] for h in heads], axis=0)
            s = _dot_nt(qg.astype(BF16), kb_ref[g, keys, :])
            s_prev = s[:, :WINDOW]
            if j == 0:
                s_prev = jnp.where(t > 0, s_prev, -jnp.inf)
            sink = _head_column([sink_ref[layer, h] for h in heads], WINDOW)
            p = _softmax(jnp.where(from_prev, s_prev, s[:, WINDOW:]), sink)
            p2 = jnp.concatenate([jnp.where(from_prev, p, 0.0), jnp.where(from_prev, 0.0, p)], axis=1)
            o = _dot(p2.astype(BF16), vb_ref[g, keys, :])
            outs.extend(o[i * WINDOW:(i + 1) * WINDOW, :] for i in range(Q_PER_KV))
        y_ref[rows, pool_w:pool_w + swa_w] = jnp.concatenate(outs, axis=1).astype(BF16)

    outs = []
    for h in range(N_X_HEADS):
        cs = slice(h * HEAD_DIM, (h + 1) * HEAD_DIM)
        p = _softmax(_dot_nt(xq_ref[:, cs].astype(BF16), mk_ref[:, cs].astype(BF16)))
        outs.append(_dot(p.astype(BF16), mv_ref[:, cs].astype(BF16)))
    y_ref[:, pool_w + swa_w:] = jnp.concatenate(outs, axis=1).astype(BF16)


def _mix_prompt(sink, xp, q, k, v, xq, mk, mv, pool_bd, pool_scale, layer, batch, seq):
    tq = TOKEN_TILE
    n_t = seq // tq
    pool_w, swa_w, kv_w, x_w = xp.shape[1], q.shape[1], k.shape[1], xq.shape[1]
    n_mem = mk.shape[1] // batch
    d_out = pool_w + swa_w + x_w

    tile = lambda w: pl.BlockSpec((tq, w), lambda b, t: (b * n_t + t, 0))

    def halo(rows, w):
        per_tile = tq // rows
        return pl.BlockSpec((rows, w), lambda b, t: (jnp.maximum((b * n_t + t) * per_tile - 1, 0), 0))

    mem = pl.BlockSpec((None, n_mem, x_w), lambda b, t: (layer, b, 0))
    return pl.pallas_call(
        functools.partial(_mix_prompt_kernel, layer=layer),
        grid=(batch, n_t),
        in_specs=[
            pl.BlockSpec(memory_space=pltpu.SMEM),
            tile(pool_w), halo(POOL_HALO, pool_w),
            tile(swa_w),
            tile(kv_w), halo(WINDOW, kv_w),
            tile(kv_w), halo(WINDOW, kv_w),
            tile(x_w),
            mem, mem,
            _resident((None, pool_w, pool_w), lambda b, t: (layer, 0, 0)),
            _resident((None, 1, pool_w), lambda b, t: (layer, 0, 0)),
        ],
        out_specs=(
            pl.BlockSpec((tq, d_out), lambda b, t: (b * n_t + t, 0)),
            pl.BlockSpec((None, WINDOW, kv_w), lambda b, t: (b, 0, 0)),
            pl.BlockSpec((None, WINDOW, kv_w), lambda b, t: (b, 0, 0)),
            pl.BlockSpec((None, POOL_HALO, pool_w), lambda b, t: (b, 0, 0)),
        ),
        out_shape=(
            jax.ShapeDtypeStruct((batch * seq, d_out), BF16),
            jax.ShapeDtypeStruct((batch, WINDOW, kv_w), F32),
            jax.ShapeDtypeStruct((batch, WINDOW, kv_w), F32),
            jax.ShapeDtypeStruct((batch, POOL_HALO, pool_w), F32),
        ),
        scratch_shapes=[
            pltpu.VMEM((tq + 2 * POOL_HALO, pool_w), F32),
            pltpu.VMEM((tq + 2 * POOL_HALO, pool_w), F32),
            pltpu.VMEM((tq + 2 * POOL_HALO, pool_w), F32),
            pltpu.VMEM((tq + 2 * POOL_HALO, pool_w), F32),
            pltpu.VMEM((N_KV_HEADS, tq + WINDOW, HEAD_DIM), BF16),
            pltpu.VMEM((N_KV_HEADS, tq + WINDOW, HEAD_DIM), BF16),
        ],
        compiler_params=_cparams("parallel", "arbitrary"),
        name="mix_prompt",
    )(sink, xp, xp, q, k, k, v, v, xq, mk, mv, pool_bd, pool_scale)


def _mix_sample_kernel(sink_ref, xp_ref, q_ref, k_ref, v_ref, xq_ref, ckt_ref, cvt_ref, st_ref,
                       cmkt_ref, cmvt_ref, pbd_ref, pscale_ref,
                       y_ref, nkt_ref, nvt_ref, npool_ref,
                       e_ref, w2_ref, w4_ref, w8_ref, *, layer, dec_seq):
    sb = ckt_ref.shape[0]
    pool_w = xp_ref.shape[1]
    swa_w = q_ref.shape[1]
    kv_w = k_ref.shape[1]
    h2 = 2 * POOL_HALO
    per_seq = lambda a: a.reshape(sb, dec_seq, a.shape[-1])
    flat = lambda a: a.reshape(a.shape[0] * a.shape[1], a.shape[2])

    xp = per_seq(xp_ref[...])
    e_ref[:, 0:POOL_HALO, :] = jnp.zeros((sb, POOL_HALO, pool_w), F32)
    e_ref[:, POOL_HALO:h2, :] = st_ref[...]
    e_ref[:, h2:, :] = xp
    sums = _pool_window_sums(e_ref, w2_ref, w4_ref, w8_ref, h2 + dec_seq)
    pos = PAST_LEN + lax.broadcasted_iota(jnp.int32, xp.shape, 1)
    delta = flat(_pool_delta(sums, xp, pos))
    y_pool = _dot(delta.astype(BF16), pbd_ref[...]) * pscale_ref[...]
    y_ref[:, 0:pool_w] = y_pool.astype(BF16)
    npool_ref[...] = e_ref[:, h2 + dec_seq - POOL_HALO:, :]

    lane = lax.broadcasted_iota(jnp.int32, (sb * kv_w, WINDOW), 1)
    keys_and_values = []
    for cache_ref, tok_ref, out_ref in ((ckt_ref, k_ref, nkt_ref), (cvt_ref, v_ref, nvt_ref)):
        old = cache_ref[...]
        tok_rows = jnp.concatenate([per_seq(tok_ref[...]), jnp.zeros((sb, WINDOW - dec_seq, kv_w), F32)], axis=1)
        new = jnp.swapaxes(tok_rows, 1, 2)
        shifted = jnp.where(lane >= WINDOW - dec_seq, pltpu.roll(flat(new), WINDOW - dec_seq, 1),
                            pltpu.roll(flat(old), WINDOW - dec_seq, 1))
        out_ref[...] = shifted.reshape(sb, kv_w, WINDOW)
        keys_and_values.append(jnp.concatenate([old, new], axis=2).astype(BF16))
    keys_t, values_t = keys_and_values

    n_keys = 2 * WINDOW
    n_rows = Q_PER_KV * dec_seq
    qi = lax.broadcasted_iota(jnp.int32, (n_rows, n_keys), 0) % dec_seq
    kj = lax.broadcasted_iota(jnp.int32, (n_rows, n_keys), 1)
    band = ((kj > qi) & (kj <= qi + WINDOW))[None]
    q = per_seq(q_ref[...])
    outs = []
    for g in range(N_KV_HEADS):
        heads = range(g * Q_PER_KV, (g + 1) * Q_PER_KV)
        rs = slice(g * HEAD_DIM, (g + 1) * HEAD_DIM)
        qg = jnp.concatenate([q[:, :, h * HEAD_DIM:(h + 1) * HEAD_DIM] for h in heads], axis=1)
        s = _bdot(qg.astype(BF16), keys_t[:, rs, :])
        sink = _head_column([sink_ref[layer, h] for h in heads], dec_seq)[None]
        p = _softmax(jnp.where(band, s, -jnp.inf), sink)
        o = _bdot_nt(p.astype(BF16), values_t[:, rs, :])
        outs.extend(o[:, i * dec_seq:(i + 1) * dec_seq, :] for i in range(Q_PER_KV))
    y_ref[:, pool_w:pool_w + swa_w] = flat(jnp.concatenate(outs, axis=2)).astype(BF16)

    xq = per_seq(xq_ref[...])
    outs = []
    for h in range(N_X_HEADS):
        rs = slice(h * HEAD_DIM, (h + 1) * HEAD_DIM)
        p = _softmax(_bdot(xq[:, :, rs].astype(BF16), cmkt_ref[:, rs, :].astype(BF16)))
        outs.append(_bdot_nt(p.astype(BF16), cmvt_ref[:, rs, :].astype(BF16)))
    y_ref[:, pool_w + swa_w:] = flat(jnp.concatenate(outs, axis=2)).astype(BF16)


def _mix_sample(sink, xp, q, k, v, xq, cache_kt, cache_vt, state16, cache_mkt, cache_mvt, pool_bd, pool_scale,
                layer, n_prompt, dec_seq):
    dec_batch = cache_kt.shape[1]
    sb = SAMPLE_SEQS
    rows = sb * dec_seq
    off = n_prompt // rows
    pool_w, swa_w, kv_w, x_w = xp.shape[1], q.shape[1], k.shape[1], xq.shape[1]
    n_mem = cache_mkt.shape[3]
    d_out = pool_w + swa_w + x_w

    tok = lambda w: pl.BlockSpec((rows, w), lambda i: (off + i, 0))
    per_seq = lambda a, w: pl.BlockSpec((None, sb, a, w), lambda i: (layer, i, 0, 0))
    out_seq = lambda a, w: pl.BlockSpec((sb, a, w), lambda i: (i, 0, 0))
    e_shape = (sb, 2 * POOL_HALO + dec_seq, pool_w)
    return pl.pallas_call(
        functools.partial(_mix_sample_kernel, layer=layer, dec_seq=dec_seq),
        grid=(dec_batch // sb,),
        in_specs=[
            pl.BlockSpec(memory_space=pltpu.SMEM),
            tok(pool_w), tok(swa_w), tok(kv_w), tok(kv_w), tok(x_w),
            per_seq(kv_w, WINDOW), per_seq(kv_w, WINDOW), per_seq(POOL_HALO, pool_w),
            per_seq(x_w, n_mem), per_seq(x_w, n_mem),
            _resident((None, pool_w, pool_w), lambda i: (layer, 0, 0)),
            _resident((None, 1, pool_w), lambda i: (layer, 0, 0)),
        ],
        out_specs=(
            pl.BlockSpec((rows, d_out), lambda i: (i, 0)),
            out_seq(kv_w, WINDOW), out_seq(kv_w, WINDOW), out_seq(POOL_HALO, pool_w),
        ),
        out_shape=(
            jax.ShapeDtypeStruct((dec_batch * dec_seq, d_out), BF16),
            jax.ShapeDtypeStruct((dec_batch, kv_w, WINDOW), F32),
            jax.ShapeDtypeStruct((dec_batch, kv_w, WINDOW), F32),
            jax.ShapeDtypeStruct((dec_batch, POOL_HALO, pool_w), F32),
        ),
        scratch_shapes=[
            pltpu.VMEM(e_shape, F32), pltpu.VMEM(e_shape, F32), pltpu.VMEM(e_shape, F32), pltpu.VMEM(e_shape, F32),
        ],
        compiler_params=_cparams("parallel"),
        name="mix_sample",
    )(sink, xp, q, k, v, xq, cache_kt, cache_vt, state16, cache_mkt, cache_mvt, pool_bd, pool_scale)


def _merge_ln_kernel(x_ref, yp_ref, ys_ref, win_ref, wbp_ref, wbs_ref, wbx_ref, wo_ref, g_ref, b_ref, o_ref,
                     *, alpha, n_first_tiles, gate_col0):
    d = x_ref.shape[1]
    in_first = pl.program_id(0) < n_first_tiles

    def finish(rows, z):
        o_ref[rows, :] = _layer_norm(z, g_ref[...], b_ref[...])

    pending = None
    for r0 in range(0, x_ref.shape[0], SUB_TILE):
        rows = slice(r0, r0 + SUB_TILE)
        x = x_ref[rows, :]
        xb = x.astype(BF16)
        y = jnp.where(in_first, yp_ref[rows, :], ys_ref[rows, :])
        c0 = 0
        merged = None
        for b, w_ref in enumerate((wbp_ref, wbs_ref, wbx_ref)):
            w = w_ref.shape[0]
            gate = jax.nn.sigmoid(_dot(xb, win_ref[:, gate_col0 + b * d:gate_col0 + (b + 1) * d]))
            term = gate * _dot(y[:, c0:c0 + w], w_ref[...])
            merged = term if merged is None else merged + term
            c0 += w
            if b == 0 and pending is not None:
                finish(*pending)
                pending = None
        pending = (rows, alpha * x + _dot(merged.astype(BF16), wo_ref[...]))
    finish(*pending)


def _merge_ln(x, y_p, y_s, w_in, w_br_pool, w_br_swa, w_br_cross, w_o, ln_g, ln_b, layer, alpha, gate_col0):
    n, d = x.shape
    tm = WIDE_TOKEN_TILE
    n_first_tiles = y_p.shape[0] // tm
    first, second = _split_rows(n_first_tiles)
    lsel = lambda i: (layer, 0, 0)
    res = lambda w: _resident((None,) + w.shape[1:], lsel)
    return pl.pallas_call(
        functools.partial(_merge_ln_kernel, alpha=alpha, n_first_tiles=n_first_tiles, gate_col0=gate_col0),
        grid=(n // tm,),
        in_specs=[
            pl.BlockSpec((tm, d), lambda i: (i, 0)),
            pl.BlockSpec((tm, y_p.shape[1]), first),
            pl.BlockSpec((tm, y_s.shape[1]), second),
            res(w_in), res(w_br_pool), res(w_br_swa), res(w_br_cross), res(w_o), res(ln_g), res(ln_b),
        ],
        out_specs=pl.BlockSpec((tm, d), lambda i: (i, 0)),
        out_shape=jax.ShapeDtypeStruct((n, d), F32),
        compiler_params=_cparams("arbitrary"),
        name="merge_ln",
    )(x, y_p, y_s, w_in, w_br_pool, w_br_swa, w_br_cross, w_o, ln_g, ln_b)


def kernel(x_prompt, x_sample, cache_swa_k, cache_swa_v, state_pool, cache_mem_k, cache_mem_v, mem_prompt, w_mem_k, w_mem_v, ffn1_w_gate, ffn1_w_up, ffn1_w_down, ln1_g, ln1_b, w_in, pool_mix, pool_scale, attn_sink, w_br_pool, w_br_swa, w_br_cross, w_o, ln2_g, ln2_b, ffn2_w_gate, ffn2_w_up, ffn2_w_down, ln3_g, ln3_b):
    batch, seq, d = x_prompt.shape
    dec_batch, dec_seq, _ = x_sample.shape
    depth = w_in.shape[0]
    n_groups, pool_group = pool_mix.shape[1], pool_mix.shape[2]
    pool_w = n_groups * pool_group
    swa_w = w_br_swa.shape[1]
    kv_w = cache_swa_k.shape[3] * cache_swa_k.shape[4]
    x_w = w_br_cross.shape[1]
    n_mem = mem_prompt.shape[1]
    widths = (pool_w, swa_w, kv_w, x_w)
    proj_w = sum(widths) + kv_w
    assert pool_group == POOL_GROUP and swa_w == N_Q_HEADS * HEAD_DIM and kv_w == N_KV_HEADS * HEAD_DIM
    assert x_w == N_X_HEADS * HEAD_DIM and cache_swa_k.shape[2] == WINDOW
    assert state_pool.shape[2] == POOL_HALO - 1 and w_in.shape[2] == proj_w + 3 * d
    alpha = (2 * depth) ** 0.25
    n_prompt, n_sample = batch * seq, dec_batch * dec_seq

    bf = lambda w: w.astype(BF16)
    vec = lambda p: p.reshape(depth, 1, p.shape[-1])
    w_in_b = bf(w_in)
    pool_bd = bf(jnp.einsum("lgcd,gh->lgchd", pool_mix, jnp.eye(n_groups, dtype=F32))
                 .reshape(depth, pool_w, pool_w))
    state16 = jnp.pad(state_pool, ((0, 0), (0, 0), (1, 0), (0, 0)))
    to_cols = lambda c: c.transpose(0, 1, 3, 4, 2).reshape(depth, dec_batch, c.shape[3] * HEAD_DIM, c.shape[2])
    from_cols = lambda c, n_h: c.reshape(depth, dec_batch, n_h, HEAD_DIM, c.shape[-1]).transpose(0, 1, 4, 2, 3)
    cache_kt, cache_vt = to_cols(cache_swa_k), to_cols(cache_swa_v)
    cache_mkt, cache_mvt = to_cols(cache_mem_k), to_cols(cache_mem_v)

    tables = _rope_tables(seq, n_sample, dec_seq)
    mk, mv = _mem_kv(bf(mem_prompt.reshape(batch * n_mem, d)), bf(w_mem_k), bf(w_mem_v))

    ffn1 = (bf(ffn1_w_gate), bf(ffn1_w_up), bf(ffn1_w_down), vec(ln1_g), vec(ln1_b))
    ffn2 = (bf(ffn2_w_gate), bf(ffn2_w_up), bf(ffn2_w_down), vec(ln3_g), vec(ln3_b))
    merge = (w_in_b, bf(w_br_pool), bf(w_br_swa), bf(w_br_cross), bf(w_o), vec(ln2_g), vec(ln2_b))
    pscale = vec(pool_scale)

    xs = (x_prompt.reshape(n_prompt, d), x_sample.reshape(n_sample, d))
    pk, pv, pp, sk, sv, sp = [], [], [], [], [], []
    for l in range(depth):
        x, xp, q, k, v, xq = _ffn(xs, *ffn1, l, alpha, n_prompt, proj=(w_in_b, tables, seq, widths))
        y_p, nk, nv, npool_p = _mix_prompt(attn_sink, xp, q, k, v, xq, mk, mv, pool_bd, pscale, l, batch, seq)
        y_s, nkt, nvt, npool_s = _mix_sample(attn_sink, xp, q, k, v, xq, cache_kt, cache_vt, state16,
                                             cache_mkt, cache_mvt, pool_bd, pscale, l, n_prompt, dec_seq)
        x = _merge_ln(x, y_p, y_s, *merge, l, alpha, proj_w)
        xs = _ffn((x,), *ffn2, l, alpha, n_prompt, split_out=(l == depth - 1))
        pk.append(nk)
        pv.append(nv)
        pp.append(npool_p[:, 1:])
        sk.append(nkt)
        sv.append(nvt)
        sp.append(npool_s[:, 1:])

    heads = lambda a, n_h: a.reshape(a.shape[:-1] + (n_h, HEAD_DIM))
    return (xs[0].reshape(batch, seq, d), xs[1].reshape(dec_batch, dec_seq, d),
            heads(jnp.stack(pk), N_KV_HEADS), heads(jnp.stack(pv), N_KV_HEADS), jnp.stack(pp),
            heads(mk.reshape(depth, batch, n_mem, x_w), N_X_HEADS),
            heads(mv.reshape(depth, batch, n_mem, x_w), N_X_HEADS),
            from_cols(jnp.stack(sk), N_KV_HEADS), from_cols(jnp.stack(sv), N_KV_HEADS), jnp.stack(sp))
```

```python
import functools

import jax
import jax.numpy as jnp
from jax import lax
from jax.experimental import pallas as pl
from jax.experimental.pallas import tpu as pltpu

F32 = jnp.float32
BF16 = jnp.bfloat16

HEAD_DIM = 64
N_Q_HEADS = 8
N_KV_HEADS = 2
Q_PER_KV = N_Q_HEADS // N_KV_HEADS
N_X_HEADS = 4
WINDOW = 128
ROPE_THETA = 500000.0
ROT_DIM = 16
ROT_HALF = ROT_DIM // 2
POOL_GROUP = 64
POOL_HALO = 16
PAST_LEN = 8192
LN_EPS = 1e-5
SM_SCALE = HEAD_DIM ** -0.5

VMEM_LIMIT_BYTES_V7X = 56 * 1024 * 1024
LANES = 128

TOKEN_TILE = 512
WIDE_TOKEN_TILE = 1024
SUB_TILE = 512
FF_CHUNK = 256
SAMPLE_SEQS = 8


def _cparams(*semantics):
    return pltpu.CompilerParams(dimension_semantics=semantics, vmem_limit_bytes=VMEM_LIMIT_BYTES_V7X)


def _resident(shape, index_map):
    return pl.BlockSpec(shape, index_map, pipeline_mode=pl.Buffered(1))


def _dot(a, b):
    return jnp.dot(a, b, preferred_element_type=F32)


def _dot_nt(a, b):
    return lax.dot_general(a, b, (((1,), (1,)), ((), ())), preferred_element_type=F32)


def _bdot_nt(a, b):
    return lax.dot_general(a, b, (((2,), (2,)), ((0,), (0,))), preferred_element_type=F32)


def _bdot(a, b):
    return lax.dot_general(a, b, (((2,), (1,)), ((0,), (0,))), preferred_element_type=F32)


def _layer_norm(z, g, b):
    mu = jnp.mean(z, axis=-1, keepdims=True)
    zc = z - mu
    var = jnp.mean(zc * zc, axis=-1, keepdims=True)
    return zc * lax.rsqrt(var + LN_EPS) * g + b


def _split_rows(n_first_tiles):
    first = lambda i: (jnp.minimum(i, n_first_tiles - 1), 0)
    second = lambda i: (jnp.maximum(i - n_first_tiles, 0), 0)
    return first, second


def _rope_table_kernel(cos_ref, sp_ref, sm_ref, *, seq, dec_seq):
    n = cos_ref.shape[0]
    r = lax.broadcasted_iota(jnp.int32, (n, LANES), 0)
    lane = lax.broadcasted_iota(jnp.int32, (n, LANES), 1)
    pos = jnp.where(r < seq, r, PAST_LEN + (r - seq) % dec_seq)
    d = lane % HEAD_DIM
    fidx = (d % ROT_HALF).astype(F32)
    inv_freq = jnp.power(ROPE_THETA, -fidx * (2.0 / ROT_DIM))
    ang = pos.astype(F32) * inv_freq
    cos = jnp.cos(ang)
    sin = jnp.sin(ang)
    cos_ref[...] = jnp.where(d < ROT_DIM, cos, 1.0)
    sp_ref[...] = jnp.where((d >= ROT_HALF) & (d < ROT_DIM), sin, 0.0)
    sm_ref[...] = jnp.where(d < ROT_HALF, -sin, 0.0)


def _rope_tables(seq, n_sample_rows, dec_seq):
    n = seq + n_sample_rows
    out = jax.ShapeDtypeStruct((n, LANES), F32)
    return pl.pallas_call(
        functools.partial(_rope_table_kernel, seq=seq, dec_seq=dec_seq),
        out_shape=(out, out, out),
        name="rope_tables",
    )()


def _mem_kv_kernel(m_ref, wk_ref, wv_ref, k_ref, v_ref, kh_ref, vt_ref):
    m = m_ref[...]
    k = _dot(m, wk_ref[...])
    v = _dot(m, wv_ref[...])
    k_ref[...] = k
    v_ref[...] = v
    batch, n_heads, n_mem, _ = kh_ref.shape
    for b in range(batch):
        rows = slice(b * n_mem, (b + 1) * n_mem)
        for h in range(n_heads):
            kh_ref[b, h] = k[rows, h * HEAD_DIM:(h + 1) * HEAD_DIM].astype(BF16)
        vt_ref[b] = v[rows, :].T.astype(BF16)


def _mem_kv(mem, w_k, w_v, batch):
    depth, d, xw = w_k.shape
    rows = mem.shape[0]
    n_mem = rows // batch
    n_heads = xw // HEAD_DIM
    out = jax.ShapeDtypeStruct((depth, rows, xw), F32)
    w_spec = pl.BlockSpec((None, d, xw), lambda l: (l, 0, 0))
    o_spec = pl.BlockSpec((None, rows, xw), lambda l: (l, 0, 0))
    return pl.pallas_call(
        _mem_kv_kernel,
        grid=(depth,),
        in_specs=[pl.BlockSpec((rows, d), lambda l: (0, 0)), w_spec, w_spec],
        out_specs=(
            o_spec, o_spec,
            pl.BlockSpec((None, batch, n_heads, n_mem, HEAD_DIM), lambda l: (l, 0, 0, 0, 0)),
            pl.BlockSpec((None, batch, xw, n_mem), lambda l: (l, 0, 0, 0)),
        ),
        out_shape=(
            out, out,
            jax.ShapeDtypeStruct((depth, batch, n_heads, n_mem, HEAD_DIM), BF16),
            jax.ShapeDtypeStruct((depth, batch, xw, n_mem), BF16),
        ),
        compiler_params=_cparams("parallel"),
        name="mem_kv",
    )(mem, w_k, w_v)


def _ffn_kernel(*refs, alpha, n_first_tiles, split_in, split_out, proj_widths):
    refs = list(refs)
    x_refs = [refs.pop(0) for _ in range(2 if split_in else 1)]
    wg_ref, wu_ref, wd_ref, g_ref, b_ref = (refs.pop(0) for _ in range(5))
    if proj_widths:
        wp_ref, cos_ref, sp_ref, sm_ref = (refs.pop(0) for _ in range(4))
    o_refs = [refs.pop(0) for _ in range(2 if split_out else 1)]
    if proj_widths:
        xp_ref, q_ref, k_ref, v_ref, xq_ref = (refs.pop(0) for _ in range(5))
    (a_ref,) = refs
    i = pl.program_id(0)
    d_ff = wg_ref.shape[1]

    def finish(rows, z):
        out = _layer_norm(z, g_ref[...], b_ref[...])

        if split_out:
            @pl.when(i < n_first_tiles)
            def _():
                o_refs[0][rows, :] = out
            o_refs[1][rows, :] = out
        else:
            o_refs[0][rows, :] = out

        if proj_widths:
            pool_w, swa_w, kv_w, x_w = proj_widths
            ob = out.astype(BF16)
            cos, sp, sm = cos_ref[rows, :], sp_ref[rows, :], sm_ref[rows, :]

            def rope(t):
                return t * cos + pltpu.roll(t, ROT_HALF, 1) * sp + pltpu.roll(t, LANES - ROT_HALF, 1) * sm

            c0 = 0
            xp_ref[rows, :] = _dot(ob, wp_ref[:, c0:c0 + pool_w])
            c0 += pool_w
            q = _dot(ob, wp_ref[:, c0:c0 + swa_w])
            for s in range(swa_w // LANES):
                slab = slice(s * LANES, (s + 1) * LANES)
                q_ref[rows, slab] = rope(q[:, slab]) * SM_SCALE
            c0 += swa_w
            kv = _dot(ob, wp_ref[:, c0:c0 + 2 * kv_w])
            k_ref[rows, :] = rope(kv[:, 0:kv_w])
            v_ref[rows, :] = kv[:, kv_w:]
            c0 += 2 * kv_w
            xq_ref[rows, :] = _dot(ob, wp_ref[:, c0:c0 + x_w]) * SM_SCALE

    pending = None
    for r0 in range(0, a_ref.shape[0], SUB_TILE):
        rows = slice(r0, r0 + SUB_TILE)
        x = x_refs[0][rows, :]
        if split_in:
            x = jnp.where(i < n_first_tiles, x, x_refs[1][rows, :])
        xb = x.astype(BF16)
        for c in range(0, d_ff, FF_CHUNK):
            g = _dot(xb, wg_ref[:, c:c + FF_CHUNK])
            u = _dot(xb, wu_ref[:, c:c + FF_CHUNK])
            a_ref[rows, c:c + FF_CHUNK] = (g * jax.nn.sigmoid(g) * u).astype(BF16)
            if c == FF_CHUNK and pending is not None:
                finish(*pending)
                pending = None
        y = _dot(a_ref[rows, :], wd_ref[...])
        pending = (rows, alpha * x + 0.5 * y)
    finish(*pending)


def _ffn(xs,w_gate, w_up, w_down, ln_g, ln_b, layer, alpha, n_prompt, split_out=False, proj=None):
    split_in = len(xs) == 2
    d = xs[0].shape[1]
    n = sum(x.shape[0] for x in xs)
    d_ff = w_gate.shape[2]
    tm = TOKEN_TILE if proj is not None else WIDE_TOKEN_TILE
    n_first_tiles = n_prompt // tm
    first, second = _split_rows(n_first_tiles)
    whole = lambda i: (i, 0)
    lsel = lambda i: (layer, 0, 0)

    in_specs = [pl.BlockSpec((tm, d), first), pl.BlockSpec((tm, d), second)] if split_in else [pl.BlockSpec((tm, d), whole)]
    in_specs += [
        _resident((None, d, d_ff), lsel), _resident((None, d, d_ff), lsel), _resident((None, d_ff, d), lsel),
        _resident((None, 1, d), lsel), _resident((None, 1, d), lsel),
    ]
    args = list(xs) + [w_gate, w_up, w_down, ln_g, ln_b]
    if split_out:
        out_specs = [pl.BlockSpec((tm, d), first), pl.BlockSpec((tm, d), second)]
        out_shape = [jax.ShapeDtypeStruct((n_prompt, d), F32), jax.ShapeDtypeStruct((n - n_prompt, d), F32)]
    else:
        out_specs = [pl.BlockSpec((tm, d), whole)]
        out_shape = [jax.ShapeDtypeStruct((n, d), F32)]
    widths = None
    if proj is not None:
        w_in, tables, seq, widths = proj
        pool_w, swa_w, kv_w, x_w = widths
        assert kv_w == LANES
        seq_tiles = seq // tm

        def table_idx(i):
            return (jnp.where(i < n_first_tiles, i % seq_tiles, seq_tiles + i - n_first_tiles), 0)

        t_spec = pl.BlockSpec((tm, LANES), table_idx)
        in_specs += [_resident((None, d, sum(widths) + kv_w), lsel), t_spec, t_spec, t_spec]
        args += [w_in, *tables]
        for w in (pool_w, swa_w, kv_w, kv_w, x_w):
            out_specs.append(pl.BlockSpec((tm, w), whole))
            out_shape.append(jax.ShapeDtypeStruct((n, w), F32))
    return pl.pallas_call(
        functools.partial(_ffn_kernel, alpha=alpha, n_first_tiles=n_first_tiles, split_in=split_in,
                          split_out=split_out, proj_widths=widths),
        grid=(n // tm,),
        in_specs=in_specs,
        out_specs=out_specs,
        out_shape=out_shape,
        scratch_shapes=[pltpu.VMEM((tm, d_ff), BF16)],
        compiler_params=_cparams("arbitrary"),
        name="ffn_proj" if proj is not None else "ffn",
    )(*args)


def _pool_window_sums(e_ref, w2_ref, w4_ref, w8_ref, n):
    h = 2 * POOL_HALO
    lead = (slice(None),) * (len(e_ref.shape) - 2)
    rows = lambda a, b: lead + (slice(a, b), slice(None))
    sl = lambda ref, a, b: ref[rows(a, b)]
    w2_ref[rows(8, n)] = sl(e_ref, 8, n) + sl(e_ref, 7, n - 1)
    w4_ref[rows(16, n)] = sl(w2_ref, 16, n) + sl(w2_ref, 14, n - 2)
    w8_ref[rows(24, n)] = sl(w4_ref, 24, n) + sl(w4_ref, 20, n - 4)
    w16 = sl(w8_ref, h, n) + sl(w8_ref, h - 8, n - 8)
    return sl(w2_ref, h, n), sl(w4_ref, h, n), sl(w8_ref, h, n), w16


def _pool_delta(sums, xp, pos):
    w2, w4, w8, w16 = sums
    grp = lax.broadcasted_iota(jnp.int32, xp.shape, xp.ndim - 1) // POOL_GROUP
    s = jnp.where(grp == 0, w2, jnp.where(grp == 1, w4, jnp.where(grp == 2, w8, w16)))
    width = jnp.left_shift(2, grp)
    cnt = jnp.minimum(width, pos + 1).astype(F32)
    return s / cnt - xp


def _per_head(values, n_per_head, axis):
    n = len(values) * n_per_head
    shape = (n, 1) if axis == 0 else (1, n)
    head = lax.broadcasted_iota(jnp.int32, shape, axis) // n_per_head
    vec = values[-1]
    for i in reversed(range(len(values) - 1)):
        vec = jnp.where(head <= i, values[i], vec)
    return vec


def _softmax(s, sink=None, axis=-1):
    m = jnp.max(s, axis=axis, keepdims=True)
    if sink is not None:
        m = jnp.maximum(m, sink)
    p = jnp.exp(s - m)
    den = jnp.sum(p, axis=axis, keepdims=True)
    if sink is not None:
        den = den + jnp.exp(sink - m)
    return p * (1.0 / den)


def _mix_prompt_kernel(sink_ref, xp_ref, xph_ref, q_ref, k_ref, kh_ref, v_ref, vh_ref, xq_ref,
                       mkh_ref, mvt_ref, pbd_ref, pscale_ref, y_ref, nk_ref, nv_ref, npool_ref,
                       e_ref, w2_ref, w4_ref, w8_ref, kb_ref, vt_ref, *, layer):
    t = pl.program_id(1)
    tq = xp_ref.shape[0]
    pool_w = xp_ref.shape[1]
    swa_w = q_ref.shape[1]

    nk_ref[...] = k_ref[tq - WINDOW:, :]
    nv_ref[...] = v_ref[tq - WINDOW:, :]
    npool_ref[...] = xp_ref[tq - POOL_HALO:, :]

    xp = xp_ref[...]
    e_ref[0:POOL_HALO, :] = jnp.zeros((POOL_HALO, pool_w), F32)
    e_ref[POOL_HALO:2 * POOL_HALO, :] = jnp.where(t == 0, 0.0, xph_ref[...])
    e_ref[2 * POOL_HALO:, :] = xp
    sums = _pool_window_sums(e_ref, w2_ref, w4_ref, w8_ref, tq + 2 * POOL_HALO)
    pos = t * tq + lax.broadcasted_iota(jnp.int32, (tq, pool_w), 0)
    delta = _pool_delta(sums, xp, pos)
    y_pool = _dot(delta.astype(BF16), pbd_ref[...]) * pscale_ref[...]
    y_ref[:, 0:pool_w] = y_pool.astype(BF16)

    def store_pair(top, bottom, rows, col0):
        y_ref[rows, col0:col0 + 2 * HEAD_DIM] = jnp.concatenate([top, bottom], axis=0).T.astype(BF16)

    for g in range(N_KV_HEADS):
        cs = slice(g * HEAD_DIM, (g + 1) * HEAD_DIM)
        kb_ref[g, 0:WINDOW, :] = kh_ref[:, cs].astype(BF16)
        kb_ref[g, WINDOW:, :] = k_ref[:, cs].astype(BF16)
    vt_ref[:, 0:WINDOW] = vh_ref[...].T.astype(BF16)
    vt_ref[:, WINDOW:] = v_ref[...].T.astype(BF16)
    n_cols = Q_PER_KV * WINDOW
    qi = lax.broadcasted_iota(jnp.int32, (WINDOW, n_cols), 1) % WINDOW
    from_prev = lax.broadcasted_iota(jnp.int32, (WINDOW, n_cols), 0) > qi
    for j in range(tq // WINDOW):
        rows = slice(j * WINDOW, (j + 1) * WINDOW)
        keys = slice(j * WINDOW, (j + 2) * WINDOW)
        for g in range(N_KV_HEADS):
            heads = range(g * Q_PER_KV, (g + 1) * Q_PER_KV)
            qg = jnp.concatenate([q_ref[rows, h * HEAD_DIM:(h + 1) * HEAD_DIM] for h in heads], axis=0)
            s = _dot_nt(kb_ref[g, keys, :], qg.astype(BF16))
            s_prev = s[:WINDOW, :]
            if j == 0:
                s_prev = jnp.where(t > 0, s_prev, -jnp.inf)
            sink = _per_head([sink_ref[layer, h] for h in heads], WINDOW, axis=1)
            p = _softmax(jnp.where(from_prev, s_prev, s[WINDOW:, :]), sink, axis=0)
            p2 = jnp.concatenate([jnp.where(from_prev, p, 0.0), jnp.where(from_prev, 0.0, p)], axis=0)
            o = _dot(vt_ref[g * HEAD_DIM:(g + 1) * HEAD_DIM, keys], p2.astype(BF16))
            for i in range(0, Q_PER_KV, 2):
                store_pair(o[:, i * WINDOW:(i + 1) * WINDOW], o[:, (i + 1) * WINDOW:(i + 2) * WINDOW],
                           rows, pool_w + (heads[0] + i) * HEAD_DIM)

    outs = []
    for h in range(N_X_HEADS):
        cs = slice(h * HEAD_DIM, (h + 1) * HEAD_DIM)
        p = _softmax(_dot_nt(mkh_ref[h], xq_ref[:, cs].astype(BF16)), axis=0)
        outs.append(_dot(mvt_ref[cs, :], p.astype(BF16)))
    for h in range(0, N_X_HEADS, 2):
        store_pair(outs[h], outs[h + 1], slice(None), pool_w + swa_w + h * HEAD_DIM)


def _mix_prompt(sink, xp, q, k, v, xq, mkh, mvt, pool_bd, pool_scale, layer, batch, seq):
    tq = TOKEN_TILE
    n_t = seq // tq
    pool_w, swa_w, kv_w, x_w = xp.shape[1], q.shape[1], k.shape[1], xq.shape[1]
    n_mem = mvt.shape[3]
    d_out = pool_w + swa_w + x_w

    tile = lambda w: pl.BlockSpec((tq, w), lambda b, t: (b * n_t + t, 0))

    def halo(rows, w):
        per_tile = tq // rows
        return pl.BlockSpec((rows, w), lambda b, t: (jnp.maximum((b * n_t + t) * per_tile - 1, 0), 0))

    mem_k = pl.BlockSpec((None, None, N_X_HEADS, n_mem, HEAD_DIM), lambda b, t: (layer, b, 0, 0, 0))
    mem_v = pl.BlockSpec((None, None, x_w, n_mem), lambda b, t: (layer, b, 0, 0))
    return pl.pallas_call(
        functools.partial(_mix_prompt_kernel, layer=layer),
        grid=(batch, n_t),
        in_specs=[
            pl.BlockSpec(memory_space=pltpu.SMEM),
            tile(pool_w), halo(POOL_HALO, pool_w),
            tile(swa_w),
            tile(kv_w), halo(WINDOW, kv_w),
            tile(kv_w), halo(WINDOW, kv_w),
            tile(x_w),
            mem_k, mem_v,
            _resident((None, pool_w, pool_w), lambda b, t: (layer, 0, 0)),
            _resident((None, 1, pool_w), lambda b, t: (layer, 0, 0)),
        ],
        out_specs=(
            pl.BlockSpec((tq, d_out), lambda b, t: (b * n_t + t, 0)),
            pl.BlockSpec((None, WINDOW, kv_w), lambda b, t: (b, 0, 0)),
            pl.BlockSpec((None, WINDOW, kv_w), lambda b, t: (b, 0, 0)),
            pl.BlockSpec((None, POOL_HALO, pool_w), lambda b, t: (b, 0, 0)),
        ),
        out_shape=(
            jax.ShapeDtypeStruct((batch * seq, d_out), BF16),
            jax.ShapeDtypeStruct((batch, WINDOW, kv_w), F32),
            jax.ShapeDtypeStruct((batch, WINDOW, kv_w), F32),
            jax.ShapeDtypeStruct((batch, POOL_HALO, pool_w), F32),
        ),
        scratch_shapes=[
            pltpu.VMEM((tq + 2 * POOL_HALO, pool_w), F32),
            pltpu.VMEM((tq + 2 * POOL_HALO, pool_w), F32),
            pltpu.VMEM((tq + 2 * POOL_HALO, pool_w), F32),
            pltpu.VMEM((tq + 2 * POOL_HALO, pool_w), F32),
            pltpu.VMEM((N_KV_HEADS, tq + WINDOW, HEAD_DIM), BF16),
            pltpu.VMEM((kv_w, tq + WINDOW), BF16),
        ],
        compiler_params=_cparams("parallel", "arbitrary"),
        name="mix_prompt",
    )(sink, xp, xp, q, k, k, v, v, xq, mkh, mvt, pool_bd, pool_scale)


def _mix_sample_kernel(sink_ref, xp_ref, q_ref, k_ref, v_ref, xq_ref, ckt_ref, cvt_ref, st_ref,
                       cmkt_ref, cmvt_ref, pbd_ref, pscale_ref,
                       y_ref, nkt_ref, nvt_ref, npool_ref,
                       e_ref, w2_ref, w4_ref, w8_ref, *, layer, dec_seq):
    sb = ckt_ref.shape[0]
    pool_w = xp_ref.shape[1]
    swa_w = q_ref.shape[1]
    kv_w = k_ref.shape[1]
    h2 = 2 * POOL_HALO
    per_seq = lambda a: a.reshape(sb, dec_seq, a.shape[-1])
    flat = lambda a: a.reshape(a.shape[0] * a.shape[1], a.shape[2])

    xp = per_seq(xp_ref[...])
    e_ref[:, 0:POOL_HALO, :] = jnp.zeros((sb, POOL_HALO, pool_w), F32)
    e_ref[:, POOL_HALO:h2, :] = st_ref[...]
    e_ref[:, h2:, :] = xp
    sums = _pool_window_sums(e_ref, w2_ref, w4_ref, w8_ref, h2 + dec_seq)
    pos = PAST_LEN + lax.broadcasted_iota(jnp.int32, xp.shape, 1)
    delta = flat(_pool_delta(sums, xp, pos))
    y_pool = _dot(delta.astype(BF16), pbd_ref[...]) * pscale_ref[...]
    y_ref[:, 0:pool_w] = y_pool.astype(BF16)
    npool_ref[...] = e_ref[:, h2 + dec_seq - POOL_HALO:, :]

    lane = lax.broadcasted_iota(jnp.int32, (sb * kv_w, WINDOW), 1)
    keys_and_values = []
    for cache_ref, tok_ref, out_ref in ((ckt_ref, k_ref, nkt_ref), (cvt_ref, v_ref, nvt_ref)):
        old = cache_ref[...]
        tok_rows = jnp.concatenate([per_seq(tok_ref[...]), jnp.zeros((sb, WINDOW - dec_seq, kv_w), F32)], axis=1)
        new = jnp.swapaxes(tok_rows, 1, 2)
        shifted = jnp.where(lane >= WINDOW - dec_seq, pltpu.roll(flat(new), WINDOW - dec_seq, 1),
                            pltpu.roll(flat(old), WINDOW - dec_seq, 1))
        out_ref[...] = shifted.reshape(sb, kv_w, WINDOW)
        keys_and_values.append(jnp.concatenate([old, new], axis=2).astype(BF16))
    keys_t, values_t = keys_and_values

    n_keys = 2 * WINDOW
    n_rows = Q_PER_KV * dec_seq
    qi = lax.broadcasted_iota(jnp.int32, (n_rows, n_keys), 0) % dec_seq
    kj = lax.broadcasted_iota(jnp.int32, (n_rows, n_keys), 1)
    band = ((kj > qi) & (kj <= qi + WINDOW))[None]
    q = per_seq(q_ref[...])
    outs = []
    for g in range(N_KV_HEADS):
        heads = range(g * Q_PER_KV, (g + 1) * Q_PER_KV)
        rs = slice(g * HEAD_DIM, (g + 1) * HEAD_DIM)
        qg = jnp.concatenate([q[:, :, h * HEAD_DIM:(h + 1) * HEAD_DIM] for h in heads], axis=1)
        s = _bdot(qg.astype(BF16), keys_t[:, rs, :])
        sink = _per_head([sink_ref[layer, h] for h in heads], dec_seq, axis=0)[None]
        p = _softmax(jnp.where(band, s, -jnp.inf), sink)
        o = _bdot_nt(p.astype(BF16), values_t[:, rs, :])
        outs.extend(o[:, i * dec_seq:(i + 1) * dec_seq, :] for i in range(Q_PER_KV))
    y_ref[:, pool_w:pool_w + swa_w] = flat(jnp.concatenate(outs, axis=2)).astype(BF16)

    xq = per_seq(xq_ref[...])
    outs = []
    for h in range(N_X_HEADS):
        rs = slice(h * HEAD_DIM, (h + 1) * HEAD_DIM)
        p = _softmax(_bdot(xq[:, :, rs].astype(BF16), cmkt_ref[:, rs, :].astype(BF16)))
        outs.append(_bdot_nt(p.astype(BF16), cmvt_ref[:, rs, :].astype(BF16)))
    y_ref[:, pool_w + swa_w:] = flat(jnp.concatenate(outs, axis=2)).astype(BF16)


def _mix_sample(sink, xp, q, k, v, xq, cache_kt, cache_vt, state16, cache_mkt, cache_mvt, pool_bd, pool_scale,
                layer, n_prompt, dec_seq):
    dec_batch = cache_kt.shape[1]
    sb = SAMPLE_SEQS
    rows = sb * dec_seq
    off = n_prompt // rows
    pool_w, swa_w, kv_w, x_w = xp.shape[1], q.shape[1], k.shape[1], xq.shape[1]
    n_mem = cache_mkt.shape[3]
    d_out = pool_w + swa_w + x_w

    tok = lambda w: pl.BlockSpec((rows, w), lambda i: (off + i, 0))
    per_seq = lambda a, w: pl.BlockSpec((None, sb, a, w), lambda i: (layer, i, 0, 0))
    out_seq = lambda a, w: pl.BlockSpec((sb, a, w), lambda i: (i, 0, 0))
    e_shape = (sb, 2 * POOL_HALO + dec_seq, pool_w)
    return pl.pallas_call(
        functools.partial(_mix_sample_kernel, layer=layer, dec_seq=dec_seq),
        grid=(dec_batch // sb,),
        in_specs=[
            pl.BlockSpec(memory_space=pltpu.SMEM),
            tok(pool_w), tok(swa_w), tok(kv_w), tok(kv_w), tok(x_w),
            per_seq(kv_w, WINDOW), per_seq(kv_w, WINDOW), per_seq(POOL_HALO, pool_w),
            per_seq(x_w, n_mem), per_seq(x_w, n_mem),
            _resident((None, pool_w, pool_w), lambda i: (layer, 0, 0)),
            _resident((None, 1, pool_w), lambda i: (layer, 0, 0)),
        ],
        out_specs=(
            pl.BlockSpec((rows, d_out), lambda i: (i, 0)),
            out_seq(kv_w, WINDOW), out_seq(kv_w, WINDOW), out_seq(POOL_HALO, pool_w),
        ),
        out_shape=(
            jax.ShapeDtypeStruct((dec_batch * dec_seq, d_out), BF16),
            jax.ShapeDtypeStruct((dec_batch, kv_w, WINDOW), F32),
            jax.ShapeDtypeStruct((dec_batch, kv_w, WINDOW), F32),
            jax.ShapeDtypeStruct((dec_batch, POOL_HALO, pool_w), F32),
        ),
        scratch_shapes=[
            pltpu.VMEM(e_shape, F32), pltpu.VMEM(e_shape, F32), pltpu.VMEM(e_shape, F32), pltpu.VMEM(e_shape, F32),
        ],
        compiler_params=_cparams("parallel"),
        name="mix_sample",
    )(sink, xp, q, k, v, xq, cache_kt, cache_vt, state16, cache_mkt, cache_mvt, pool_bd, pool_scale)


def _merge_ln_kernel(x_ref, yp_ref, ys_ref, win_ref, wbp_ref, wbs_ref, wbx_ref, wo_ref, g_ref, b_ref, o_ref,
                     *, alpha, n_first_tiles, gate_col0):
    d = x_ref.shape[1]
    in_first = pl.program_id(0) < n_first_tiles

    def finish(rows, z):
        o_ref[rows, :] = _layer_norm(z, g_ref[...], b_ref[...])

    pending = None
    for r0 in range(0, x_ref.shape[0], SUB_TILE):
        rows = slice(r0, r0 + SUB_TILE)
        x = x_ref[rows, :]
        xb = x.astype(BF16)
        y = jnp.where(in_first, yp_ref[rows, :], ys_ref[rows, :])
        c0 = 0
        merged = None
        for b, w_ref in enumerate((wbp_ref, wbs_ref, wbx_ref)):
            w = w_ref.shape[0]
            gate = jax.nn.sigmoid(_dot(xb, win_ref[:, gate_col0 + b * d:gate_col0 + (b + 1) * d]))
            term = gate * _dot(y[:, c0:c0 + w], w_ref[...])
            merged = term if merged is None else merged + term
            c0 += w
            if b == 0 and pending is not None:
                finish(*pending)
                pending = None
        pending = (rows, alpha * x + _dot(merged.astype(BF16), wo_ref[...]))
    finish(*pending)


def _merge_ln(x, y_p, y_s, w_in, w_br_pool, w_br_swa, w_br_cross, w_o, ln_g, ln_b, layer, alpha, gate_col0):
    n, d = x.shape
    tm = WIDE_TOKEN_TILE
    n_first_tiles = y_p.shape[0] // tm
    first, second = _split_rows(n_first_tiles)
    lsel = lambda i: (layer, 0, 0)
    res = lambda w: _resident((None,) + w.shape[1:], lsel)
    return pl.pallas_call(
        functools.partial(_merge_ln_kernel, alpha=alpha, n_first_tiles=n_first_tiles, gate_col0=gate_col0),
        grid=(n // tm,),
        in_specs=[
            pl.BlockSpec((tm, d), lambda i: (i, 0)),
            pl.BlockSpec((tm, y_p.shape[1]), first),
            pl.BlockSpec((tm, y_s.shape[1]), second),
            res(w_in), res(w_br_pool), res(w_br_swa), res(w_br_cross), res(w_o), res(ln_g), res(ln_b),
        ],
        out_specs=pl.BlockSpec((tm, d), lambda i: (i, 0)),
        out_shape=jax.ShapeDtypeStruct((n, d), F32),
        compiler_params=_cparams("arbitrary"),
        name="merge_ln",
    )(x, y_p, y_s, w_in, w_br_pool, w_br_swa, w_br_cross, w_o, ln_g, ln_b)


def kernel(x_prompt, x_sample, cache_swa_k, cache_swa_v, state_pool, cache_mem_k, cache_mem_v, mem_prompt, w_mem_k, w_mem_v, ffn1_w_gate, ffn1_w_up, ffn1_w_down, ln1_g, ln1_b, w_in, pool_mix, pool_scale, attn_sink, w_br_pool, w_br_swa, w_br_cross, w_o, ln2_g, ln2_b, ffn2_w_gate, ffn2_w_up, ffn2_w_down, ln3_g, ln3_b):
    batch, seq, d = x_prompt.shape
    dec_batch, dec_seq, _ = x_sample.shape
    depth = w_in.shape[0]
    n_groups, pool_group = pool_mix.shape[1], pool_mix.shape[2]
    pool_w = n_groups * pool_group
    swa_w = w_br_swa.shape[1]
    kv_w = cache_swa_k.shape[3] * cache_swa_k.shape[4]
    x_w = w_br_cross.shape[1]
    n_mem = mem_prompt.shape[1]
    widths = (pool_w, swa_w, kv_w, x_w)
    proj_w = sum(widths) + kv_w
    assert pool_group == POOL_GROUP and swa_w == N_Q_HEADS * HEAD_DIM and kv_w == N_KV_HEADS * HEAD_DIM
    assert x_w == N_X_HEADS * HEAD_DIM and cache_swa_k.shape[2] == WINDOW
    assert state_pool.shape[2] == POOL_HALO - 1 and w_in.shape[2] == proj_w + 3 * d
    alpha = (2 * depth) ** 0.25
    n_prompt, n_sample = batch * seq, dec_batch * dec_seq

    bf = lambda w: w.astype(BF16)
    vec = lambda p: p.reshape(depth, 1, p.shape[-1])
    w_in_b = bf(w_in)
    pool_bd = bf(jnp.einsum("lgcd,gh->lgchd", pool_mix, jnp.eye(n_groups, dtype=F32))
                 .reshape(depth, pool_w, pool_w))
    state16 = jnp.pad(state_pool, ((0, 0), (0, 0), (1, 0), (0, 0)))
    to_cols = lambda c: c.transpose(0, 1, 3, 4, 2).reshape(depth, dec_batch, c.shape[3] * HEAD_DIM, c.shape[2])
    from_cols = lambda c, n_h: c.reshape(depth, dec_batch, n_h, HEAD_DIM, c.shape[-1]).transpose(0, 1, 4, 2, 3)
    cache_kt, cache_vt = to_cols(cache_swa_k), to_cols(cache_swa_v)
    cache_mkt, cache_mvt = to_cols(cache_mem_k), to_cols(cache_mem_v)

    tables = _rope_tables(seq, n_sample, dec_seq)
    mk, mv, mkh, mvt = _mem_kv(bf(mem_prompt.reshape(batch * n_mem, d)), bf(w_mem_k), bf(w_mem_v), batch)

    ffn1 = (bf(ffn1_w_gate), bf(ffn1_w_up), bf(ffn1_w_down), vec(ln1_g), vec(ln1_b))
    ffn2 = (bf(ffn2_w_gate), bf(ffn2_w_up), bf(ffn2_w_down), vec(ln3_g), vec(ln3_b))
    merge = (w_in_b, bf(w_br_pool), bf(w_br_swa), bf(w_br_cross), bf(w_o), vec(ln2_g), vec(ln2_b))
    pscale = vec(pool_scale)

    xs = (x_prompt.reshape(n_prompt, d), x_sample.reshape(n_sample, d))
    pk, pv, pp, sk, sv, sp = [], [], [], [], [], []
    for l in range(depth):
        x, xp, q, k, v, xq = _ffn(xs, *ffn1, l, alpha, n_prompt, proj=(w_in_b, tables, seq, widths))
        y_p, nk, nv, npool_p = _mix_prompt(attn_sink, xp, q, k, v, xq, mkh, mvt, pool_bd, pscale, l, batch, seq)
        y_s, nkt, nvt, npool_s = _mix_sample(attn_sink, xp, q, k, v, xq, cache_kt, cache_vt, state16,
                                             cache_mkt, cache_mvt, pool_bd, pscale, l, n_prompt, dec_seq)
        x = _merge_ln(x, y_p, y_s, *merge, l, alpha, proj_w)
        xs = _ffn((x,), *ffn2, l, alpha, n_prompt, split_out=(l == depth - 1))
        pk.append(nk)
        pv.append(nv)
        pp.append(npool_p[:, 1:])
        sk.append(nkt)
        sv.append(nvt)
        sp.append(npool_s[:, 1:])

    heads = lambda a, n_h: a.reshape(a.shape[:-1] + (n_h, HEAD_DIM))
    return (xs[0].reshape(batch, seq, d), xs[1].reshape(dec_batch, dec_seq, d),
            heads(jnp.stack(pk), N_KV_HEADS), heads(jnp.stack(pv), N_KV_HEADS), jnp.stack(pp),
            heads(mk.reshape(depth, batch, n_mem, x_w), N_X_HEADS),
            heads(mv.reshape(depth, batch, n_mem, x_w), N_X_HEADS),
            from_cols(jnp.stack(sk), N_KV_HEADS), from_cols(jnp.stack(sv), N_KV_HEADS), jnp.stack(sp))
```

```python
import functools

import jax
import jax.numpy as jnp
from jax import lax
from jax.experimental import pallas as pl
from jax.experimental.pallas import tpu as pltpu

F32 = jnp.float32
BF16 = jnp.bfloat16

HEAD_DIM = 64
N_Q_HEADS = 8
N_KV_HEADS = 2
Q_PER_KV = N_Q_HEADS // N_KV_HEADS
N_X_HEADS = 4
WINDOW = 128
ROPE_THETA = 500000.0
ROT_DIM = 16
ROT_HALF = ROT_DIM // 2
POOL_GROUP = 64
POOL_HALO = 16
PAST_LEN = 8192
LN_EPS = 1e-5
SM_SCALE = HEAD_DIM ** -0.5

VMEM_LIMIT_BYTES_V7X = 56 * 1024 * 1024
LANES = 128

TOKEN_TILE = 512
WIDE_TOKEN_TILE = 1024
SUB_TILE = 256
FF_CHUNK = 256
SAMPLE_SEQS = 16


def _cparams(*semantics):
    return pltpu.CompilerParams(dimension_semantics=semantics, vmem_limit_bytes=VMEM_LIMIT_BYTES_V7X)


def _resident(shape, index_map):
    return pl.BlockSpec(shape, index_map, pipeline_mode=pl.Buffered(1))


def _dot(a, b):
    return jnp.dot(a, b, preferred_element_type=F32)


def _dot_nt(a, b):
    return lax.dot_general(a, b, (((1,), (1,)), ((), ())), preferred_element_type=F32)


def _bdot_nt(a, b):
    return lax.dot_general(a, b, (((2,), (2,)), ((0,), (0,))), preferred_element_type=F32)


def _bdot(a, b):
    return lax.dot_general(a, b, (((2,), (1,)), ((0,), (0,))), preferred_element_type=F32)


def _layer_norm(z, g, b):
    mu = jnp.mean(z, axis=-1, keepdims=True)
    zc = z - mu
    var = jnp.mean(zc * zc, axis=-1, keepdims=True)
    return zc * lax.rsqrt(var + LN_EPS) * g + b


def _split_rows(n_first_tiles):
    first = lambda i: (jnp.minimum(i, n_first_tiles - 1), 0)
    second = lambda i: (jnp.maximum(i - n_first_tiles, 0), 0)
    return first, second


def _rope_table_kernel(cos_ref, sp_ref, sm_ref, *, seq, dec_seq):
    n = cos_ref.shape[0]
    r = lax.broadcasted_iota(jnp.int32, (n, LANES), 0)
    lane = lax.broadcasted_iota(jnp.int32, (n, LANES), 1)
    pos = jnp.where(r < seq, r, PAST_LEN + (r - seq) % dec_seq)
    d = lane % HEAD_DIM
    fidx = (d % ROT_HALF).astype(F32)
    inv_freq = jnp.power(ROPE_THETA, -fidx * (2.0 / ROT_DIM))
    ang = pos.astype(F32) * inv_freq
    cos = jnp.cos(ang)
    sin = jnp.sin(ang)
    cos_ref[...] = jnp.where(d < ROT_DIM, cos, 1.0)
    sp_ref[...] = jnp.where((d >= ROT_HALF) & (d < ROT_DIM), sin, 0.0)
    sm_ref[...] = jnp.where(d < ROT_HALF, -sin, 0.0)


def _rope_tables(seq, n_sample_rows, dec_seq):
    n = seq + n_sample_rows
    out = jax.ShapeDtypeStruct((n, LANES), F32)
    return pl.pallas_call(
        functools.partial(_rope_table_kernel, seq=seq, dec_seq=dec_seq),
        out_shape=(out, out, out),
        name="rope_tables",
    )()


def _mem_kv_kernel(m_ref, wk_ref, wv_ref, k_ref, v_ref, kh_ref, vt_ref):
    m = m_ref[...]
    k = _dot(m, wk_ref[...])
    v = _dot(m, wv_ref[...])
    k_ref[...] = k
    v_ref[...] = v
    batch, n_heads, n_mem, _ = kh_ref.shape
    for b in range(batch):
        rows = slice(b * n_mem, (b + 1) * n_mem)
        for h in range(n_heads):
            kh_ref[b, h] = k[rows, h * HEAD_DIM:(h + 1) * HEAD_DIM].astype(BF16)
        vt_ref[b] = v[rows, :].T.astype(BF16)


def _mem_kv(mem, w_k, w_v, batch):
    depth, d, xw = w_k.shape
    rows = mem.shape[0]
    n_mem = rows // batch
    n_heads = xw // HEAD_DIM
    out = jax.ShapeDtypeStruct((depth, rows, xw), F32)
    w_spec = pl.BlockSpec((None, d, xw), lambda l: (l, 0, 0))
    o_spec = pl.BlockSpec((None, rows, xw), lambda l: (l, 0, 0))
    return pl.pallas_call(
        _mem_kv_kernel,
        grid=(depth,),
        in_specs=[pl.BlockSpec((rows, d), lambda l: (0, 0)), w_spec, w_spec],
        out_specs=(
            o_spec, o_spec,
            pl.BlockSpec((None, batch, n_heads, n_mem, HEAD_DIM), lambda l: (l, 0, 0, 0, 0)),
            pl.BlockSpec((None, batch, xw, n_mem), lambda l: (l, 0, 0, 0)),
        ),
        out_shape=(
            out, out,
            jax.ShapeDtypeStruct((depth, batch, n_heads, n_mem, HEAD_DIM), BF16),
            jax.ShapeDtypeStruct((depth, batch, xw, n_mem), BF16),
        ),
        compiler_params=_cparams("parallel"),
        name="mem_kv",
    )(mem, w_k, w_v)


def _ffn_kernel(*refs, alpha, n_first_tiles, split_in, split_out, proj_widths):
    refs = list(refs)
    x_refs = [refs.pop(0) for _ in range(2 if split_in else 1)]
    wg_ref, wu_ref, wd_ref, g_ref, b_ref = (refs.pop(0) for _ in range(5))
    if proj_widths:
        wp_ref, cos_ref, sp_ref, sm_ref = (refs.pop(0) for _ in range(4))
    o_refs = [refs.pop(0) for _ in range(2 if split_out else 1)]
    if proj_widths:
        xp_ref, q_ref, k_ref, v_ref, xq_ref = (refs.pop(0) for _ in range(5))
    (a_ref,) = refs
    i = pl.program_id(0)
    d_ff = wg_ref.shape[1]

    def finish(rows, z):
        out = _layer_norm(z, g_ref[...], b_ref[...])

        if split_out:
            @pl.when(i < n_first_tiles)
            def _():
                o_refs[0][rows, :] = out
            o_refs[1][rows, :] = out
        else:
            o_refs[0][rows, :] = out

        if proj_widths:
            pool_w, swa_w, kv_w, x_w = proj_widths
            ob = out.astype(BF16)
            cos, sp, sm = cos_ref[rows, :], sp_ref[rows, :], sm_ref[rows, :]

            def rope(t):
                return t * cos + pltpu.roll(t, ROT_HALF, 1) * sp + pltpu.roll(t, LANES - ROT_HALF, 1) * sm

            c0 = 0
            xp_ref[rows, :] = _dot(ob, wp_ref[:, c0:c0 + pool_w])
            c0 += pool_w
            q = _dot(ob, wp_ref[:, c0:c0 + swa_w])
            for s in range(swa_w // LANES):
                slab = slice(s * LANES, (s + 1) * LANES)
                q_ref[rows, slab] = rope(q[:, slab]) * SM_SCALE
            c0 += swa_w
            kv = _dot(ob, wp_ref[:, c0:c0 + 2 * kv_w])
            k_ref[rows, :] = rope(kv[:, 0:kv_w])
            v_ref[rows, :] = kv[:, kv_w:]
            c0 += 2 * kv_w
            xq_ref[rows, :] = _dot(ob, wp_ref[:, c0:c0 + x_w]) * SM_SCALE

    pending = None
    for r0 in range(0, a_ref.shape[0], SUB_TILE):
        rows = slice(r0, r0 + SUB_TILE)
        x = x_refs[0][rows, :]
        if split_in:
            x = jnp.where(i < n_first_tiles, x, x_refs[1][rows, :])
        xb = x.astype(BF16)
        for c in range(0, d_ff, FF_CHUNK):
            g = _dot(xb, wg_ref[:, c:c + FF_CHUNK])
            u = _dot(xb, wu_ref[:, c:c + FF_CHUNK])
            a_ref[rows, c:c + FF_CHUNK] = (g * jax.nn.sigmoid(g) * u).astype(BF16)
            if c == FF_CHUNK and pending is not None:
                finish(*pending)
                pending = None
        y = _dot(a_ref[rows, :], wd_ref[...])
        pending = (rows, alpha * x + 0.5 * y)
    finish(*pending)


def _ffn(xs,w_gate, w_up, w_down, ln_g, ln_b, layer, alpha, n_prompt, split_out=False, proj=None):
    split_in = len(xs) == 2
    d = xs[0].shape[1]
    n = sum(x.shape[0] for x in xs)
    d_ff = w_gate.shape[2]
    tm = TOKEN_TILE if proj is not None else WIDE_TOKEN_TILE
    n_first_tiles = n_prompt // tm
    first, second = _split_rows(n_first_tiles)
    whole = lambda i: (i, 0)
    lsel = lambda i: (layer, 0, 0)

    in_specs = [pl.BlockSpec((tm, d), first), pl.BlockSpec((tm, d), second)] if split_in else [pl.BlockSpec((tm, d), whole)]
    in_specs += [
        _resident((None, d, d_ff), lsel), _resident((None, d, d_ff), lsel), _resident((None, d_ff, d), lsel),
        _resident((None, 1, d), lsel), _resident((None, 1, d), lsel),
    ]
    args = list(xs) + [w_gate, w_up, w_down, ln_g, ln_b]
    if split_out:
        out_specs = [pl.BlockSpec((tm, d), first), pl.BlockSpec((tm, d), second)]
        out_shape = [jax.ShapeDtypeStruct((n_prompt, d), F32), jax.ShapeDtypeStruct((n - n_prompt, d), F32)]
    else:
        out_specs = [pl.BlockSpec((tm, d), whole)]
        out_shape = [jax.ShapeDtypeStruct((n, d), F32)]
    widths = None
    if proj is not None:
        w_in, tables, seq, widths = proj
        pool_w, swa_w, kv_w, x_w = widths
        assert kv_w == LANES
        seq_tiles = seq // tm

        def table_idx(i):
            return (jnp.where(i < n_first_tiles, i % seq_tiles, seq_tiles + i - n_first_tiles), 0)

        t_spec = pl.BlockSpec((tm, LANES), table_idx)
        in_specs += [_resident((None, d, sum(widths) + kv_w), lsel), t_spec, t_spec, t_spec]
        args += [w_in, *tables]
        for w in (pool_w, swa_w, kv_w, kv_w, x_w):
            out_specs.append(pl.BlockSpec((tm, w), whole))
            out_shape.append(jax.ShapeDtypeStruct((n, w), F32))
    return pl.pallas_call(
        functools.partial(_ffn_kernel, alpha=alpha, n_first_tiles=n_first_tiles, split_in=split_in,
                          split_out=split_out, proj_widths=widths),
        grid=(n // tm,),
        in_specs=in_specs,
        out_specs=out_specs,
        out_shape=out_shape,
        scratch_shapes=[pltpu.VMEM((tm, d_ff), BF16)],
        compiler_params=_cparams("arbitrary"),
        name="ffn_proj" if proj is not None else "ffn",
    )(*args)


def _pool_window_sums(e_ref, w2_ref, w4_ref, w8_ref, n):
    h = 2 * POOL_HALO
    lead = (slice(None),) * (len(e_ref.shape) - 2)
    rows = lambda a, b: lead + (slice(a, b), slice(None))
    sl = lambda ref, a, b: ref[rows(a, b)]
    w2_ref[rows(8, n)] = sl(e_ref, 8, n) + sl(e_ref, 7, n - 1)
    w4_ref[rows(16, n)] = sl(w2_ref, 16, n) + sl(w2_ref, 14, n - 2)
    w8_ref[rows(24, n)] = sl(w4_ref, 24, n) + sl(w4_ref, 20, n - 4)
    w16 = sl(w8_ref, h, n) + sl(w8_ref, h - 8, n - 8)
    return sl(w2_ref, h, n), sl(w4_ref, h, n), sl(w8_ref, h, n), w16


def _pool_delta(sums, xp, pos):
    w2, w4, w8, w16 = sums
    grp = lax.broadcasted_iota(jnp.int32, xp.shape, xp.ndim - 1) // POOL_GROUP
    s = jnp.where(grp == 0, w2, jnp.where(grp == 1, w4, jnp.where(grp == 2, w8, w16)))
    width = jnp.left_shift(2, grp)
    cnt = jnp.minimum(width, pos + 1).astype(F32)
    return s / cnt - xp


def _per_head(values, n_per_head, axis):
    n = len(values) * n_per_head
    shape = (n, 1) if axis == 0 else (1, n)
    head = lax.broadcasted_iota(jnp.int32, shape, axis) // n_per_head
    vec = values[-1]
    for i in reversed(range(len(values) - 1)):
        vec = jnp.where(head <= i, values[i], vec)
    return vec


def _softmax(s, sink=None, axis=-1):
    m = jnp.max(s, axis=axis, keepdims=True)
    if sink is not None:
        m = jnp.maximum(m, sink)
    p = jnp.exp(s - m)
    den = jnp.sum(p, axis=axis, keepdims=True)
    if sink is not None:
        den = den + jnp.exp(sink - m)
    return p * (1.0 / den)


def _mix_prompt_kernel(sink_ref, xp_ref, xph_ref, q_ref, k_ref, kh_ref, v_ref, vh_ref, xq_ref,
                       mkh_ref, mvt_ref, pbd_ref, pscale_ref, y_ref, nk_ref, nv_ref, npool_ref,
                       e_ref, w2_ref, w4_ref, w8_ref, kb_ref, vt_ref, *, layer):
    t = pl.program_id(1)
    tq = xp_ref.shape[0]
    pool_w = xp_ref.shape[1]
    swa_w = q_ref.shape[1]

    nk_ref[...] = k_ref[tq - WINDOW:, :]
    nv_ref[...] = v_ref[tq - WINDOW:, :]
    npool_ref[...] = xp_ref[tq - POOL_HALO:, :]

    xp = xp_ref[...]
    e_ref[0:POOL_HALO, :] = jnp.zeros((POOL_HALO, pool_w), F32)
    e_ref[POOL_HALO:2 * POOL_HALO, :] = jnp.where(t == 0, 0.0, xph_ref[...])
    e_ref[2 * POOL_HALO:, :] = xp
    sums = _pool_window_sums(e_ref, w2_ref, w4_ref, w8_ref, tq + 2 * POOL_HALO)
    pos = t * tq + lax.broadcasted_iota(jnp.int32, (tq, pool_w), 0)
    delta = _pool_delta(sums, xp, pos)
    y_pool = _dot(delta.astype(BF16), pbd_ref[...]) * pscale_ref[...]
    y_ref[:, 0:pool_w] = y_pool.astype(BF16)

    def store_pair(top, bottom, rows, col0):
        y_ref[rows, col0:col0 + 2 * HEAD_DIM] = jnp.concatenate([top, bottom], axis=0).T.astype(BF16)

    for g in range(N_KV_HEADS):
        cs = slice(g * HEAD_DIM, (g + 1) * HEAD_DIM)
        kb_ref[g, 0:WINDOW, :] = kh_ref[:, cs].astype(BF16)
        kb_ref[g, WINDOW:, :] = k_ref[:, cs].astype(BF16)
    vt_ref[:, 0:WINDOW] = vh_ref[...].T.astype(BF16)
    vt_ref[:, WINDOW:] = v_ref[...].T.astype(BF16)
    n_cols = Q_PER_KV * WINDOW
    qi = lax.broadcasted_iota(jnp.int32, (WINDOW, n_cols), 1) % WINDOW
    from_prev = lax.broadcasted_iota(jnp.int32, (WINDOW, n_cols), 0) > qi
    for j in range(tq // WINDOW):
        rows = slice(j * WINDOW, (j + 1) * WINDOW)
        keys = slice(j * WINDOW, (j + 2) * WINDOW)
        for g in range(N_KV_HEADS):
            heads = range(g * Q_PER_KV, (g + 1) * Q_PER_KV)
            qg = jnp.concatenate([q_ref[rows, h * HEAD_DIM:(h + 1) * HEAD_DIM] for h in heads], axis=0)
            s = _dot_nt(kb_ref[g, keys, :], qg.astype(BF16))
            s_prev = s[:WINDOW, :]
            if j == 0:
                s_prev = jnp.where(t > 0, s_prev, -jnp.inf)
            sink = _per_head([sink_ref[layer, h] for h in heads], WINDOW, axis=1)
            p = _softmax(jnp.where(from_prev, s_prev, s[WINDOW:, :]), sink, axis=0)
            p2 = jnp.concatenate([jnp.where(from_prev, p, 0.0), jnp.where(from_prev, 0.0, p)], axis=0)
            o = _dot(vt_ref[g * HEAD_DIM:(g + 1) * HEAD_DIM, keys], p2.astype(BF16))
            for i in range(0, Q_PER_KV, 2):
                store_pair(o[:, i * WINDOW:(i + 1) * WINDOW], o[:, (i + 1) * WINDOW:(i + 2) * WINDOW],
                           rows, pool_w + (heads[0] + i) * HEAD_DIM)

    outs = []
    for h in range(N_X_HEADS):
        cs = slice(h * HEAD_DIM, (h + 1) * HEAD_DIM)
        p = _softmax(_dot_nt(mkh_ref[h], xq_ref[:, cs].astype(BF16)), axis=0)
        outs.append(_dot(mvt_ref[cs, :], p.astype(BF16)))
    for h in range(0, N_X_HEADS, 2):
        store_pair(outs[h], outs[h + 1], slice(None), pool_w + swa_w + h * HEAD_DIM)


def _mix_prompt(sink, xp, q, k, v, xq, mkh, mvt, pool_bd, pool_scale, layer, batch, seq):
    tq = TOKEN_TILE
    n_t = seq // tq
    pool_w, swa_w, kv_w, x_w = xp.shape[1], q.shape[1], k.shape[1], xq.shape[1]
    n_mem = mvt.shape[3]
    d_out = pool_w + swa_w + x_w

    tile = lambda w: pl.BlockSpec((tq, w), lambda b, t: (b * n_t + t, 0))

    def halo(rows, w):
        per_tile = tq // rows
        return pl.BlockSpec((rows, w), lambda b, t: (jnp.maximum((b * n_t + t) * per_tile - 1, 0), 0))

    mem_k = pl.BlockSpec((None, None, N_X_HEADS, n_mem, HEAD_DIM), lambda b, t: (layer, b, 0, 0, 0))
    mem_v = pl.BlockSpec((None, None, x_w, n_mem), lambda b, t: (layer, b, 0, 0))
    return pl.pallas_call(
        functools.partial(_mix_prompt_kernel, layer=layer),
        grid=(batch, n_t),
        in_specs=[
            pl.BlockSpec(memory_space=pltpu.SMEM),
            tile(pool_w), halo(POOL_HALO, pool_w),
            tile(swa_w),
            tile(kv_w), halo(WINDOW, kv_w),
            tile(kv_w), halo(WINDOW, kv_w),
            tile(x_w),
            mem_k, mem_v,
            _resident((None, pool_w, pool_w), lambda b, t: (layer, 0, 0)),
            _resident((None, 1, pool_w), lambda b, t: (layer, 0, 0)),
        ],
        out_specs=(
            pl.BlockSpec((tq, d_out), lambda b, t: (b * n_t + t, 0)),
            pl.BlockSpec((None, WINDOW, kv_w), lambda b, t: (b, 0, 0)),
            pl.BlockSpec((None, WINDOW, kv_w), lambda b, t: (b, 0, 0)),
            pl.BlockSpec((None, POOL_HALO, pool_w), lambda b, t: (b, 0, 0)),
        ),
        out_shape=(
            jax.ShapeDtypeStruct((batch * seq, d_out), BF16),
            jax.ShapeDtypeStruct((batch, WINDOW, kv_w), F32),
            jax.ShapeDtypeStruct((batch, WINDOW, kv_w), F32),
            jax.ShapeDtypeStruct((batch, POOL_HALO, pool_w), F32),
        ),
        scratch_shapes=[
            pltpu.VMEM((tq + 2 * POOL_HALO, pool_w), F32),
            pltpu.VMEM((tq + 2 * POOL_HALO, pool_w), F32),
            pltpu.VMEM((tq + 2 * POOL_HALO, pool_w), F32),
            pltpu.VMEM((tq + 2 * POOL_HALO, pool_w), F32),
            pltpu.VMEM((N_KV_HEADS, tq + WINDOW, HEAD_DIM), BF16),
            pltpu.VMEM((kv_w, tq + WINDOW), BF16),
        ],
        compiler_params=_cparams("parallel", "arbitrary"),
        name="mix_prompt",
    )(sink, xp, xp, q, k, k, v, v, xq, mkh, mvt, pool_bd, pool_scale)


def _mix_sample_kernel(sink_ref, xp_ref, q_ref, k_ref, v_ref, xq_ref, ckt_ref, cvt_ref, st_ref,
                       cmkt_ref, cmvt_ref, pbd_ref, pscale_ref,
                       y_ref, nkt_ref, nvt_ref, npool_ref,
                       e_ref, w2_ref, w4_ref, w8_ref, *, layer, dec_seq):
    sb = ckt_ref.shape[0]
    pool_w = xp_ref.shape[1]
    swa_w = q_ref.shape[1]
    kv_w = k_ref.shape[1]
    h2 = 2 * POOL_HALO
    per_seq = lambda a: a.reshape(sb, dec_seq, a.shape[-1])
    flat = lambda a: a.reshape(a.shape[0] * a.shape[1], a.shape[2])

    xp = per_seq(xp_ref[...])
    e_ref[:, 0:POOL_HALO, :] = jnp.zeros((sb, POOL_HALO, pool_w), F32)
    e_ref[:, POOL_HALO:h2, :] = st_ref[...]
    e_ref[:, h2:, :] = xp
    sums = _pool_window_sums(e_ref, w2_ref, w4_ref, w8_ref, h2 + dec_seq)
    pos = PAST_LEN + lax.broadcasted_iota(jnp.int32, xp.shape, 1)
    delta = flat(_pool_delta(sums, xp, pos))
    y_pool = _dot(delta.astype(BF16), pbd_ref[...]) * pscale_ref[...]
    y_ref[:, 0:pool_w] = y_pool.astype(BF16)
    npool_ref[...] = e_ref[:, h2 + dec_seq - POOL_HALO:, :]

    lane = lax.broadcasted_iota(jnp.int32, (sb * kv_w, WINDOW), 1)
    keys_and_values = []
    for cache_ref, tok_ref, out_ref in ((ckt_ref, k_ref, nkt_ref), (cvt_ref, v_ref, nvt_ref)):
        old = cache_ref[...]
        tok_rows = jnp.concatenate([per_seq(tok_ref[...]), jnp.zeros((sb, WINDOW - dec_seq, kv_w), F32)], axis=1)
        new = jnp.swapaxes(tok_rows, 1, 2)
        shifted = jnp.where(lane >= WINDOW - dec_seq, pltpu.roll(flat(new), WINDOW - dec_seq, 1),
                            pltpu.roll(flat(old), WINDOW - dec_seq, 1))
        out_ref[...] = shifted.reshape(sb, kv_w, WINDOW)
        keys_and_values.append(jnp.concatenate([old, new], axis=2).astype(BF16))
    keys_t, values_t = keys_and_values

    n_keys = 2 * WINDOW
    n_rows = Q_PER_KV * dec_seq
    qi = lax.broadcasted_iota(jnp.int32, (n_rows, n_keys), 0) % dec_seq
    kj = lax.broadcasted_iota(jnp.int32, (n_rows, n_keys), 1)
    band = ((kj > qi) & (kj <= qi + WINDOW))[None]
    q = per_seq(q_ref[...])
    outs = []
    for g in range(N_KV_HEADS):
        heads = range(g * Q_PER_KV, (g + 1) * Q_PER_KV)
        rs = slice(g * HEAD_DIM, (g + 1) * HEAD_DIM)
        qg = jnp.concatenate([q[:, :, h * HEAD_DIM:(h + 1) * HEAD_DIM] for h in heads], axis=1)
        s = _bdot(qg.astype(BF16), keys_t[:, rs, :])
        sink = _per_head([sink_ref[layer, h] for h in heads], dec_seq, axis=0)[None]
        p = _softmax(jnp.where(band, s, -jnp.inf), sink)
        o = _bdot_nt(p.astype(BF16), values_t[:, rs, :])
        outs.extend(o[:, i * dec_seq:(i + 1) * dec_seq, :] for i in range(Q_PER_KV))
    y_ref[:, pool_w:pool_w + swa_w] = flat(jnp.concatenate(outs, axis=2)).astype(BF16)

    xq = per_seq(xq_ref[...])
    outs = []
    for h in range(N_X_HEADS):
        rs = slice(h * HEAD_DIM, (h + 1) * HEAD_DIM)
        p = _softmax(_bdot(xq[:, :, rs].astype(BF16), cmkt_ref[:, rs, :].astype(BF16)))
        outs.append(_bdot_nt(p.astype(BF16), cmvt_ref[:, rs, :].astype(BF16)))
    y_ref[:, pool_w + swa_w:] = flat(jnp.concatenate(outs, axis=2)).astype(BF16)


def _mix_sample(sink, xp, q, k, v, xq, cache_kt, cache_vt, state16, cache_mkt, cache_mvt, pool_bd, pool_scale,
                layer, n_prompt, dec_seq):
    dec_batch = cache_kt.shape[1]
    sb = SAMPLE_SEQS
    rows = sb * dec_seq
    off = n_prompt // rows
    pool_w, swa_w, kv_w, x_w = xp.shape[1], q.shape[1], k.shape[1], xq.shape[1]
    n_mem = cache_mkt.shape[3]
    d_out = pool_w + swa_w + x_w

    tok = lambda w: pl.BlockSpec((rows, w), lambda i: (off + i, 0))
    per_seq = lambda a, w: pl.BlockSpec((None, sb, a, w), lambda i: (layer, i, 0, 0))
    out_seq = lambda a, w: pl.BlockSpec((sb, a, w), lambda i: (i, 0, 0))
    e_shape = (sb, 2 * POOL_HALO + dec_seq, pool_w)
    return pl.pallas_call(
        functools.partial(_mix_sample_kernel, layer=layer, dec_seq=dec_seq),
        grid=(dec_batch // sb,),
        in_specs=[
            pl.BlockSpec(memory_space=pltpu.SMEM),
            tok(pool_w), tok(swa_w), tok(kv_w), tok(kv_w), tok(x_w),
            per_seq(kv_w, WINDOW), per_seq(kv_w, WINDOW), per_seq(POOL_HALO, pool_w),
            per_seq(x_w, n_mem), per_seq(x_w, n_mem),
            _resident((None, pool_w, pool_w), lambda i: (layer, 0, 0)),
            _resident((None, 1, pool_w), lambda i: (layer, 0, 0)),
        ],
        out_specs=(
            pl.BlockSpec((rows, d_out), lambda i: (i, 0)),
            out_seq(kv_w, WINDOW), out_seq(kv_w, WINDOW), out_seq(POOL_HALO, pool_w),
        ),
        out_shape=(
            jax.ShapeDtypeStruct((dec_batch * dec_seq, d_out), BF16),
            jax.ShapeDtypeStruct((dec_batch, kv_w, WINDOW), F32),
            jax.ShapeDtypeStruct((dec_batch, kv_w, WINDOW), F32),
            jax.ShapeDtypeStruct((dec_batch, POOL_HALO, pool_w), F32),
        ),
        scratch_shapes=[
            pltpu.VMEM(e_shape, F32), pltpu.VMEM(e_shape, F32), pltpu.VMEM(e_shape, F32), pltpu.VMEM(e_shape, F32),
        ],
        compiler_params=_cparams("parallel"),
        name="mix_sample",
    )(sink, xp, q, k, v, xq, cache_kt, cache_vt, state16, cache_mkt, cache_mvt, pool_bd, pool_scale)


def _merge_ln_kernel(x_ref, yp_ref, ys_ref, win_ref, wbp_ref, wbs_ref, wbx_ref, wo_ref, g_ref, b_ref, o_ref,
                     *, alpha, n_first_tiles, gate_col0):
    d = x_ref.shape[1]
    in_first = pl.program_id(0) < n_first_tiles

    def finish(rows, z):
        o_ref[rows, :] = _layer_norm(z, g_ref[...], b_ref[...])

    pending = None
    for r0 in range(0, x_ref.shape[0], SUB_TILE):
        rows = slice(r0, r0 + SUB_TILE)
        x = x_ref[rows, :]
        xb = x.astype(BF16)
        y = jnp.where(in_first, yp_ref[rows, :], ys_ref[rows, :])
        c0 = 0
        merged = None
        for b, w_ref in enumerate((wbp_ref, wbs_ref, wbx_ref)):
            w = w_ref.shape[0]
            gate = jax.nn.sigmoid(_dot(xb, win_ref[:, gate_col0 + b * d:gate_col0 + (b + 1) * d]))
            term = gate * _dot(y[:, c0:c0 + w], w_ref[...])
            merged = term if merged is None else merged + term
            c0 += w
            if b == 0 and pending is not None:
                finish(*pending)
                pending = None
        pending = (rows, alpha * x + _dot(merged.astype(BF16), wo_ref[...]))
    finish(*pending)


def _merge_ln(x, y_p, y_s, w_in, w_br_pool, w_br_swa, w_br_cross, w_o, ln_g, ln_b, layer, alpha, gate_col0):
    n, d = x.shape
    tm = WIDE_TOKEN_TILE
    n_first_tiles = y_p.shape[0] // tm
    first, second = _split_rows(n_first_tiles)
    lsel = lambda i: (layer, 0, 0)
    res = lambda w: _resident((None,) + w.shape[1:], lsel)
    return pl.pallas_call(
        functools.partial(_merge_ln_kernel, alpha=alpha, n_first_tiles=n_first_tiles, gate_col0=gate_col0),
        grid=(n // tm,),
        in_specs=[
            pl.BlockSpec((tm, d), lambda i: (i, 0)),
            pl.BlockSpec((tm, y_p.shape[1]), first),
            pl.BlockSpec((tm, y_s.shape[1]), second),
            res(w_in), res(w_br_pool), res(w_br_swa), res(w_br_cross), res(w_o), res(ln_g), res(ln_b),
        ],
        out_specs=pl.BlockSpec((tm, d), lambda i: (i, 0)),
        out_shape=jax.ShapeDtypeStruct((n, d), F32),
        compiler_params=_cparams("arbitrary"),
        name="merge_ln",
    )(x, y_p, y_s, w_in, w_br_pool, w_br_swa, w_br_cross, w_o, ln_g, ln_b)


def kernel(x_prompt, x_sample, cache_swa_k, cache_swa_v, state_pool, cache_mem_k, cache_mem_v, mem_prompt, w_mem_k, w_mem_v, ffn1_w_gate, ffn1_w_up, ffn1_w_down, ln1_g, ln1_b, w_in, pool_mix, pool_scale, attn_sink, w_br_pool, w_br_swa, w_br_cross, w_o, ln2_g, ln2_b, ffn2_w_gate, ffn2_w_up, ffn2_w_down, ln3_g, ln3_b):
    batch, seq, d = x_prompt.shape
    dec_batch, dec_seq, _ = x_sample.shape
    depth = w_in.shape[0]
    n_groups, pool_group = pool_mix.shape[1], pool_mix.shape[2]
    pool_w = n_groups * pool_group
    swa_w = w_br_swa.shape[1]
    kv_w = cache_swa_k.shape[3] * cache_swa_k.shape[4]
    x_w = w_br_cross.shape[1]
    n_mem = mem_prompt.shape[1]
    widths = (pool_w, swa_w, kv_w, x_w)
    proj_w = sum(widths) + kv_w
    assert pool_group == POOL_GROUP and swa_w == N_Q_HEADS * HEAD_DIM and kv_w == N_KV_HEADS * HEAD_DIM
    assert x_w == N_X_HEADS * HEAD_DIM and cache_swa_k.shape[2] == WINDOW
    assert state_pool.shape[2] == POOL_HALO - 1 and w_in.shape[2] == proj_w + 3 * d
    alpha = (2 * depth) ** 0.25
    n_prompt, n_sample = batch * seq, dec_batch * dec_seq

    bf = lambda w: w.astype(BF16)
    vec = lambda p: p.reshape(depth, 1, p.shape[-1])
    w_in_b = bf(w_in)
    pool_bd = bf(jnp.einsum("lgcd,gh->lgchd", pool_mix, jnp.eye(n_groups, dtype=F32))
                 .reshape(depth, pool_w, pool_w))
    state16 = jnp.pad(state_pool, ((0, 0), (0, 0), (1, 0), (0, 0)))
    to_cols = lambda c: c.transpose(0, 1, 3, 4, 2).reshape(depth, dec_batch, c.shape[3] * HEAD_DIM, c.shape[2])
    from_cols = lambda c, n_h: c.reshape(depth, dec_batch, n_h, HEAD_DIM, c.shape[-1]).transpose(0, 1, 4, 2, 3)
    cache_kt, cache_vt = to_cols(cache_swa_k), to_cols(cache_swa_v)
    cache_mkt, cache_mvt = to_cols(cache_mem_k), to_cols(cache_mem_v)

    tables = _rope_tables(seq, n_sample, dec_seq)
    mk, mv, mkh, mvt = _mem_kv(bf(mem_prompt.reshape(batch * n_mem, d)), bf(w_mem_k), bf(w_mem_v), batch)

    ffn1 = (bf(ffn1_w_gate), bf(ffn1_w_up), bf(ffn1_w_down), vec(ln1_g), vec(ln1_b))
    ffn2 = (bf(ffn2_w_gate), bf(ffn2_w_up), bf(ffn2_w_down), vec(ln3_g), vec(ln3_b))
    merge = (w_in_b, bf(w_br_pool), bf(w_br_swa), bf(w_br_cross), bf(w_o), vec(ln2_g), vec(ln2_b))
    pscale = vec(pool_scale)

    xs = (x_prompt.reshape(n_prompt, d), x_sample.reshape(n_sample, d))
    pk, pv, pp, sk, sv, sp = [], [], [], [], [], []
    for l in range(depth):
        x, xp, q, k, v, xq = _ffn(xs, *ffn1, l, alpha, n_prompt, proj=(w_in_b, tables, seq, widths))
        y_p, nk, nv, npool_p = _mix_prompt(attn_sink, xp, q, k, v, xq, mkh, mvt, pool_bd, pscale, l, batch, seq)
        y_s, nkt, nvt, npool_s = _mix_sample(attn_sink, xp, q, k, v, xq, cache_kt, cache_vt, state16,
                                             cache_mkt, cache_mvt, pool_bd, pscale, l, n_prompt, dec_seq)
        x = _merge_ln(x, y_p, y_s, *merge, l, alpha, proj_w)
        xs = _ffn((x,), *ffn2, l, alpha, n_prompt, split_out=(l == depth - 1))
        pk.append(nk)
        pv.append(nv)
        pp.append(npool_p[:, 1:])
        sk.append(nkt)
        sv.append(nvt)
        sp.append(npool_s[:, 1:])

    heads = lambda a, n_h: a.reshape(a.shape[:-1] + (n_h, HEAD_DIM))
    return (xs[0].reshape(batch, seq, d), xs[1].reshape(dec_batch, dec_seq, d),
            heads(jnp.stack(pk), N_KV_HEADS), heads(jnp.stack(pv), N_KV_HEADS), jnp.stack(pp),
            heads(mk.reshape(depth, batch, n_mem, x_w), N_X_HEADS),
            heads(mv.reshape(depth, batch, n_mem, x_w), N_X_HEADS),
            from_cols(jnp.stack(sk), N_KV_HEADS), from_cols(jnp.stack(sv), N_KV_HEADS), jnp.stack(sp))
```

```python
import functools

import jax
import jax.numpy as jnp
from jax import lax
from jax.experimental import pallas as pl
from jax.experimental.pallas import tpu as pltpu

F32 = jnp.float32
BF16 = jnp.bfloat16

HEAD_DIM = 64
N_Q_HEADS = 8
N_KV_HEADS = 2
Q_PER_KV = N_Q_HEADS // N_KV_HEADS
N_X_HEADS = 4
WINDOW = 128
ROPE_THETA = 500000.0
ROT_DIM = 16
ROT_HALF = ROT_DIM // 2
POOL_GROUP = 64
POOL_HALO = 16
PAST_LEN = 8192
LN_EPS = 1e-5
SM_SCALE = HEAD_DIM ** -0.5

VMEM_LIMIT_BYTES_V7X = 56 * 1024 * 1024
LANES = 128

TOKEN_TILE = 512
WIDE_TOKEN_TILE = 1024
SUB_TILE = 256
FF_CHUNK = 256
SAMPLE_SEQS = 16


def _cparams(*semantics):
    return pltpu.CompilerParams(dimension_semantics=semantics, vmem_limit_bytes=VMEM_LIMIT_BYTES_V7X)


def _resident(shape, index_map):
    return pl.BlockSpec(shape, index_map, pipeline_mode=pl.Buffered(1))


def _dot(a, b):
    return jnp.dot(a, b, preferred_element_type=F32)


def _dot_nt(a, b):
    return lax.dot_general(a, b, (((1,), (1,)), ((), ())), preferred_element_type=F32)


def _bdot_nt(a, b):
    return lax.dot_general(a, b, (((2,), (2,)), ((0,), (0,))), preferred_element_type=F32)


def _bdot(a, b):
    return lax.dot_general(a, b, (((2,), (1,)), ((0,), (0,))), preferred_element_type=F32)


def _layer_norm(z, g, b):
    mu = jnp.mean(z, axis=-1, keepdims=True)
    zc = z - mu
    var = jnp.mean(zc * zc, axis=-1, keepdims=True)
    return zc * lax.rsqrt(var + LN_EPS) * g + b


N_CAST_BLOCKS = 16


def _cast_specs(weights, layer, block_of_step):
    in_specs, out_specs, out_shape = [], [], []
    for w in weights:
        _, r, c = w.shape
        rb = r // N_CAST_BLOCKS
        assert rb * N_CAST_BLOCKS == r and rb % 16 == 0
        in_specs.append(pl.BlockSpec((None, rb, c), lambda *ids: (layer, block_of_step(*ids), 0)))
        out_specs.append(pl.BlockSpec((rb, c), lambda *ids: (block_of_step(*ids), 0)))
        out_shape.append(jax.ShapeDtypeStruct((r, c), BF16))
    return in_specs, out_specs, out_shape


def _cast_blocks(src_refs, dst_refs):
    for src, dst in zip(src_refs, dst_refs):
        dst[...] = src[...].astype(BF16)


def _split_rows(n_first_tiles):
    first = lambda i: (jnp.minimum(i, n_first_tiles - 1), 0)
    second = lambda i: (jnp.maximum(i - n_first_tiles, 0), 0)
    return first, second


def _rope_table_kernel(cos_ref, sp_ref, sm_ref, *, seq, dec_seq):
    n = cos_ref.shape[0]
    r = lax.broadcasted_iota(jnp.int32, (n, LANES), 0)
    lane = lax.broadcasted_iota(jnp.int32, (n, LANES), 1)
    pos = jnp.where(r < seq, r, PAST_LEN + (r - seq) % dec_seq)
    d = lane % HEAD_DIM
    fidx = (d % ROT_HALF).astype(F32)
    inv_freq = jnp.power(ROPE_THETA, -fidx * (2.0 / ROT_DIM))
    ang = pos.astype(F32) * inv_freq
    cos = jnp.cos(ang)
    sin = jnp.sin(ang)
    cos_ref[...] = jnp.where(d < ROT_DIM, cos, 1.0)
    sp_ref[...] = jnp.where((d >= ROT_HALF) & (d < ROT_DIM), sin, 0.0)
    sm_ref[...] = jnp.where(d < ROT_HALF, -sin, 0.0)


def _rope_tables(seq, n_sample_rows, dec_seq):
    n = seq + n_sample_rows
    out = jax.ShapeDtypeStruct((n, LANES), F32)
    return pl.pallas_call(
        functools.partial(_rope_table_kernel, seq=seq, dec_seq=dec_seq),
        out_shape=(out, out, out),
        name="rope_tables",
    )()


def _mem_kv_kernel(m_ref, wk_ref, wv_ref, k_ref, v_ref, kh_ref, vt_ref):
    m = m_ref[...]
    k = _dot(m, wk_ref[...])
    v = _dot(m, wv_ref[...])
    k_ref[...] = k
    v_ref[...] = v
    batch, n_heads, n_mem, _ = kh_ref.shape
    for b in range(batch):
        rows = slice(b * n_mem, (b + 1) * n_mem)
        for h in range(n_heads):
            kh_ref[b, h] = k[rows, h * HEAD_DIM:(h + 1) * HEAD_DIM].astype(BF16)
        vt_ref[b] = v[rows, :].T.astype(BF16)


def _mem_kv(mem, w_k, w_v, batch):
    depth, d, xw = w_k.shape
    rows = mem.shape[0]
    n_mem = rows // batch
    n_heads = xw // HEAD_DIM
    out = jax.ShapeDtypeStruct((depth, rows, xw), F32)
    w_spec = pl.BlockSpec((None, d, xw), lambda l: (l, 0, 0))
    o_spec = pl.BlockSpec((None, rows, xw), lambda l: (l, 0, 0))
    return pl.pallas_call(
        _mem_kv_kernel,
        grid=(depth,),
        in_specs=[pl.BlockSpec((rows, d), lambda l: (0, 0)), w_spec, w_spec],
        out_specs=(
            o_spec, o_spec,
            pl.BlockSpec((None, batch, n_heads, n_mem, HEAD_DIM), lambda l: (l, 0, 0, 0, 0)),
            pl.BlockSpec((None, batch, xw, n_mem), lambda l: (l, 0, 0, 0)),
        ),
        out_shape=(
            out, out,
            jax.ShapeDtypeStruct((depth, batch, n_heads, n_mem, HEAD_DIM), BF16),
            jax.ShapeDtypeStruct((depth, batch, xw, n_mem), BF16),
        ),
        compiler_params=_cparams("parallel"),
        name="mem_kv",
    )(mem, w_k, w_v)


def _ffn_kernel(*refs, alpha, n_first_tiles, split_in, split_out, proj_widths):
    refs = list(refs)
    x_refs = [refs.pop(0) for _ in range(2 if split_in else 1)]
    wg_ref, wu_ref, wd_ref, g_ref, b_ref = (refs.pop(0) for _ in range(5))
    if proj_widths:
        wp_ref, cos_ref, sp_ref, sm_ref = (refs.pop(0) for _ in range(4))
    o_refs = [refs.pop(0) for _ in range(2 if split_out else 1)]
    if proj_widths:
        xp_ref, q_ref, k_ref, v_ref, xq_ref = (refs.pop(0) for _ in range(5))
    (a_ref,) = refs
    i = pl.program_id(0)
    d_ff = wg_ref.shape[1]

    def finish(rows, z):
        out = _layer_norm(z, g_ref[...], b_ref[...])

        if split_out:
            @pl.when(i < n_first_tiles)
            def _():
                o_refs[0][rows, :] = out
            o_refs[1][rows, :] = out
        else:
            o_refs[0][rows, :] = out

        if proj_widths:
            pool_w, swa_w, kv_w, x_w = proj_widths
            ob = out.astype(BF16)
            cos, sp, sm = cos_ref[rows, :], sp_ref[rows, :], sm_ref[rows, :]

            def rope(t):
                return t * cos + pltpu.roll(t, ROT_HALF, 1) * sp + pltpu.roll(t, LANES - ROT_HALF, 1) * sm

            c0 = 0
            xp_ref[rows, :] = _dot(ob, wp_ref[:, c0:c0 + pool_w])
            c0 += pool_w
            q = _dot(ob, wp_ref[:, c0:c0 + swa_w])
            for s in range(swa_w // LANES):
                slab = slice(s * LANES, (s + 1) * LANES)
                q_ref[rows, slab] = rope(q[:, slab]) * SM_SCALE
            c0 += swa_w
            kv = _dot(ob, wp_ref[:, c0:c0 + 2 * kv_w])
            k_ref[rows, :] = rope(kv[:, 0:kv_w])
            v_ref[rows, :] = kv[:, kv_w:]
            c0 += 2 * kv_w
            xq_ref[rows, :] = _dot(ob, wp_ref[:, c0:c0 + x_w]) * SM_SCALE

    pending = None
    for r0 in range(0, a_ref.shape[0], SUB_TILE):
        rows = slice(r0, r0 + SUB_TILE)
        x = x_refs[0][rows, :]
        if split_in:
            x = jnp.where(i < n_first_tiles, x, x_refs[1][rows, :])
        xb = x.astype(BF16)
        for c in range(0, d_ff, FF_CHUNK):
            g = _dot(xb, wg_ref[:, c:c + FF_CHUNK])
            u = _dot(xb, wu_ref[:, c:c + FF_CHUNK])
            a_ref[rows, c:c + FF_CHUNK] = (g * jax.nn.sigmoid(g) * u).astype(BF16)
            if c == FF_CHUNK and pending is not None:
                finish(*pending)
                pending = None
        y = _dot(a_ref[rows, :], wd_ref[...])
        pending = (rows, alpha * x + 0.5 * y)
    finish(*pending)


def _ffn(xs,w_gate, w_up, w_down, ln_g, ln_b, layer, alpha, n_prompt, split_out=False, proj=None):
    split_in = len(xs) == 2
    d = xs[0].shape[1]
    n = sum(x.shape[0] for x in xs)
    d_ff = w_gate.shape[1]
    tm = TOKEN_TILE if proj is not None else WIDE_TOKEN_TILE
    n_first_tiles = n_prompt // tm
    first, second = _split_rows(n_first_tiles)
    whole = lambda i: (i, 0)
    lsel = lambda i: (layer, 0, 0)

    in_specs = [pl.BlockSpec((tm, d), first), pl.BlockSpec((tm, d), second)] if split_in else [pl.BlockSpec((tm, d), whole)]
    in_specs += [
        _resident((d, d_ff), lambda i: (0, 0)), _resident((d, d_ff), lambda i: (0, 0)),
        _resident((d_ff, d), lambda i: (0, 0)),
        _resident((None, 1, d), lsel), _resident((None, 1, d), lsel),
    ]
    args = list(xs) + [w_gate, w_up, w_down, ln_g, ln_b]
    if split_out:
        out_specs = [pl.BlockSpec((tm, d), first), pl.BlockSpec((tm, d), second)]
        out_shape = [jax.ShapeDtypeStruct((n_prompt, d), F32), jax.ShapeDtypeStruct((n - n_prompt, d), F32)]
    else:
        out_specs = [pl.BlockSpec((tm, d), whole)]
        out_shape = [jax.ShapeDtypeStruct((n, d), F32)]
    widths = None
    if proj is not None:
        w_in, tables, seq, widths = proj
        pool_w, swa_w, kv_w, x_w = widths
        assert kv_w == LANES
        seq_tiles = seq // tm

        def table_idx(i):
            return (jnp.where(i < n_first_tiles, i % seq_tiles, seq_tiles + i - n_first_tiles), 0)

        t_spec = pl.BlockSpec((tm, LANES), table_idx)
        in_specs += [_resident((None, d, sum(widths) + kv_w), lsel), t_spec, t_spec, t_spec]
        args += [w_in, *tables]
        for w in (pool_w, swa_w, kv_w, kv_w, x_w):
            out_specs.append(pl.BlockSpec((tm, w), whole))
            out_shape.append(jax.ShapeDtypeStruct((n, w), F32))
    return pl.pallas_call(
        functools.partial(_ffn_kernel, alpha=alpha, n_first_tiles=n_first_tiles, split_in=split_in,
                          split_out=split_out, proj_widths=widths),
        grid=(n // tm,),
        in_specs=in_specs,
        out_specs=out_specs,
        out_shape=out_shape,
        scratch_shapes=[pltpu.VMEM((tm, d_ff), BF16)],
        compiler_params=_cparams("arbitrary"),
        name="ffn_proj" if proj is not None else "ffn",
    )(*args)


def _pool_window_sums(e_ref, w2_ref, w4_ref, w8_ref, n):
    h = 2 * POOL_HALO
    lead = (slice(None),) * (len(e_ref.shape) - 2)
    rows = lambda a, b: lead + (slice(a, b), slice(None))
    sl = lambda ref, a, b: ref[rows(a, b)]
    w2_ref[rows(8, n)] = sl(e_ref, 8, n) + sl(e_ref, 7, n - 1)
    w4_ref[rows(16, n)] = sl(w2_ref, 16, n) + sl(w2_ref, 14, n - 2)
    w8_ref[rows(24, n)] = sl(w4_ref, 24, n) + sl(w4_ref, 20, n - 4)
    w16 = sl(w8_ref, h, n) + sl(w8_ref, h - 8, n - 8)
    return sl(w2_ref, h, n), sl(w4_ref, h, n), sl(w8_ref, h, n), w16


def _pool_delta(sums, xp, pos):
    w2, w4, w8, w16 = sums
    grp = lax.broadcasted_iota(jnp.int32, xp.shape, xp.ndim - 1) // POOL_GROUP
    s = jnp.where(grp == 0, w2, jnp.where(grp == 1, w4, jnp.where(grp == 2, w8, w16)))
    width = jnp.left_shift(2, grp)
    cnt = jnp.minimum(width, pos + 1).astype(F32)
    return s / cnt - xp


def _per_head(values, n_per_head, axis):
    n = len(values) * n_per_head
    shape = (n, 1) if axis == 0 else (1, n)
    head = lax.broadcasted_iota(jnp.int32, shape, axis) // n_per_head
    vec = values[-1]
    for i in reversed(range(len(values) - 1)):
        vec = jnp.where(head <= i, values[i], vec)
    return vec


def _softmax(s, sink=None, axis=-1):
    m = jnp.max(s, axis=axis, keepdims=True)
    if sink is not None:
        m = jnp.maximum(m, sink)
    p = jnp.exp(s - m)
    den = jnp.sum(p, axis=axis, keepdims=True)
    if sink is not None:
        den = den + jnp.exp(sink - m)
    return p * (1.0 / den)


def _mix_prompt_kernel(sink_ref, xp_ref, xph_ref, q_ref, k_ref, kh_ref, v_ref, vh_ref, xq_ref,
                       mkh_ref, mvt_ref, pbd_ref, pscale_ref, *refs, layer):
    n_cast = (len(refs) - 10) // 2
    y_ref, nk_ref, nv_ref, npool_ref = refs[n_cast:n_cast + 4]
    e_ref, w2_ref, w4_ref, w8_ref, kb_ref, vt_ref = refs[2 * n_cast + 4:]
    _cast_blocks(refs[:n_cast], refs[n_cast + 4:2 * n_cast + 4])
    t = pl.program_id(1)
    tq = xp_ref.shape[0]
    pool_w = xp_ref.shape[1]
    swa_w = q_ref.shape[1]

    nk_ref[...] = k_ref[tq - WINDOW:, :]
    nv_ref[...] = v_ref[tq - WINDOW:, :]
    npool_ref[...] = xp_ref[tq - POOL_HALO:, :]

    xp = xp_ref[...]
    e_ref[0:POOL_HALO, :] = jnp.zeros((POOL_HALO, pool_w), F32)
    e_ref[POOL_HALO:2 * POOL_HALO, :] = jnp.where(t == 0, 0.0, xph_ref[...])
    e_ref[2 * POOL_HALO:, :] = xp
    sums = _pool_window_sums(e_ref, w2_ref, w4_ref, w8_ref, tq + 2 * POOL_HALO)
    pos = t * tq + lax.broadcasted_iota(jnp.int32, (tq, pool_w), 0)
    delta = _pool_delta(sums, xp, pos)
    y_pool = _dot(delta.astype(BF16), pbd_ref[...]) * pscale_ref[...]
    y_ref[:, 0:pool_w] = y_pool.astype(BF16)

    def store_pair(top, bottom, rows, col0):
        y_ref[rows, col0:col0 + 2 * HEAD_DIM] = jnp.concatenate([top, bottom], axis=0).T.astype(BF16)

    for g in range(N_KV_HEADS):
        cs = slice(g * HEAD_DIM, (g + 1) * HEAD_DIM)
        kb_ref[g, 0:WINDOW, :] = kh_ref[:, cs].astype(BF16)
        kb_ref[g, WINDOW:, :] = k_ref[:, cs].astype(BF16)
    vt_ref[:, 0:WINDOW] = vh_ref[...].T.astype(BF16)
    vt_ref[:, WINDOW:] = v_ref[...].T.astype(BF16)
    n_cols = Q_PER_KV * WINDOW
    qi = lax.broadcasted_iota(jnp.int32, (WINDOW, n_cols), 1) % WINDOW
    from_prev = lax.broadcasted_iota(jnp.int32, (WINDOW, n_cols), 0) > qi
    for j in range(tq // WINDOW):
        rows = slice(j * WINDOW, (j + 1) * WINDOW)
        keys = slice(j * WINDOW, (j + 2) * WINDOW)
        for g in range(N_KV_HEADS):
            heads = range(g * Q_PER_KV, (g + 1) * Q_PER_KV)
            qg = jnp.concatenate([q_ref[rows, h * HEAD_DIM:(h + 1) * HEAD_DIM] for h in heads], axis=0)
            s = _dot_nt(kb_ref[g, keys, :], qg.astype(BF16))
            s_prev = s[:WINDOW, :]
            if j == 0:
                s_prev = jnp.where(t > 0, s_prev, -jnp.inf)
            sink = _per_head([sink_ref[layer, h] for h in heads], WINDOW, axis=1)
            p = _softmax(jnp.where(from_prev, s_prev, s[WINDOW:, :]), sink, axis=0)
            p2 = jnp.concatenate([jnp.where(from_prev, p, 0.0), jnp.where(from_prev, 0.0, p)], axis=0)
            o = _dot(vt_ref[g * HEAD_DIM:(g + 1) * HEAD_DIM, keys], p2.astype(BF16))
            for i in range(0, Q_PER_KV, 2):
                store_pair(o[:, i * WINDOW:(i + 1) * WINDOW], o[:, (i + 1) * WINDOW:(i + 2) * WINDOW],
                           rows, pool_w + (heads[0] + i) * HEAD_DIM)

    outs = []
    for h in range(N_X_HEADS):
        cs = slice(h * HEAD_DIM, (h + 1) * HEAD_DIM)
        p = _softmax(_dot_nt(mkh_ref[h], xq_ref[:, cs].astype(BF16)), axis=0)
        outs.append(_dot(mvt_ref[cs, :], p.astype(BF16)))
    for h in range(0, N_X_HEADS, 2):
        store_pair(outs[h], outs[h + 1], slice(None), pool_w + swa_w + h * HEAD_DIM)


def _mix_prompt(sink, xp, q, k, v, xq, mkh, mvt, pool_bd, pool_scale, layer, batch, seq,
                cast_weights=(), cast_layer=0):
    tq = TOKEN_TILE
    n_t = seq // tq
    pool_w, swa_w, kv_w, x_w = xp.shape[1], q.shape[1], k.shape[1], xq.shape[1]
    n_mem = mvt.shape[3]
    d_out = pool_w + swa_w + x_w

    tile = lambda w: pl.BlockSpec((tq, w), lambda b, t: (b * n_t + t, 0))

    def halo(rows, w):
        per_tile = tq // rows
        return pl.BlockSpec((rows, w), lambda b, t: (jnp.maximum((b * n_t + t) * per_tile - 1, 0), 0))

    mem_k = pl.BlockSpec((None, None, N_X_HEADS, n_mem, HEAD_DIM), lambda b, t: (layer, b, 0, 0, 0))
    mem_v = pl.BlockSpec((None, None, x_w, n_mem), lambda b, t: (layer, b, 0, 0))
    steps_per_block = (batch * n_t) // N_CAST_BLOCKS
    assert steps_per_block * N_CAST_BLOCKS == batch * n_t
    cast_in, cast_out, cast_shape = _cast_specs(cast_weights, cast_layer,
                                                lambda b, t: (b * n_t + t) // steps_per_block)
    out = pl.pallas_call(
        functools.partial(_mix_prompt_kernel, layer=layer),
        grid=(batch, n_t),
        in_specs=[
            pl.BlockSpec(memory_space=pltpu.SMEM),
            tile(pool_w), halo(POOL_HALO, pool_w),
            tile(swa_w),
            tile(kv_w), halo(WINDOW, kv_w),
            tile(kv_w), halo(WINDOW, kv_w),
            tile(x_w),
            mem_k, mem_v,
            _resident((None, pool_w, pool_w), lambda b, t: (layer, 0, 0)),
            _resident((None, 1, pool_w), lambda b, t: (layer, 0, 0)),
        ] + cast_in,
        out_specs=[
            pl.BlockSpec((tq, d_out), lambda b, t: (b * n_t + t, 0)),
            pl.BlockSpec((None, WINDOW, kv_w), lambda b, t: (b, 0, 0)),
            pl.BlockSpec((None, WINDOW, kv_w), lambda b, t: (b, 0, 0)),
            pl.BlockSpec((None, POOL_HALO, pool_w), lambda b, t: (b, 0, 0)),
        ] + cast_out,
        out_shape=[
            jax.ShapeDtypeStruct((batch * seq, d_out), BF16),
            jax.ShapeDtypeStruct((batch, WINDOW, kv_w), F32),
            jax.ShapeDtypeStruct((batch, WINDOW, kv_w), F32),
            jax.ShapeDtypeStruct((batch, POOL_HALO, pool_w), F32),
        ] + cast_shape,
        scratch_shapes=[
            pltpu.VMEM((tq + 2 * POOL_HALO, pool_w), F32),
            pltpu.VMEM((tq + 2 * POOL_HALO, pool_w), F32),
            pltpu.VMEM((tq + 2 * POOL_HALO, pool_w), F32),
            pltpu.VMEM((tq + 2 * POOL_HALO, pool_w), F32),
            pltpu.VMEM((N_KV_HEADS, tq + WINDOW, HEAD_DIM), BF16),
            pltpu.VMEM((kv_w, tq + WINDOW), BF16),
        ],
        compiler_params=_cparams("arbitrary", "arbitrary"),
        name="mix_prompt",
    )(sink, xp, xp, q, k, k, v, v, xq, mkh, mvt, pool_bd, pool_scale, *cast_weights)
    return out[0], out[1], out[2], out[3], tuple(out[4:])


def _mix_sample_kernel(sink_ref, xp_ref, q_ref, k_ref, v_ref, xq_ref, ckt_ref, cvt_ref, st_ref,
                       cmkt_ref, cmvt_ref, pbd_ref, pscale_ref,
                       y_ref, nkt_ref, nvt_ref, npool_ref,
                       e_ref, w2_ref, w4_ref, w8_ref, *, layer, dec_seq):
    sb = ckt_ref.shape[0]
    pool_w = xp_ref.shape[1]
    swa_w = q_ref.shape[1]
    kv_w = k_ref.shape[1]
    h2 = 2 * POOL_HALO
    per_seq = lambda a: a.reshape(sb, dec_seq, a.shape[-1])
    flat = lambda a: a.reshape(a.shape[0] * a.shape[1], a.shape[2])

    xp = per_seq(xp_ref[...])
    e_ref[:, 0:POOL_HALO, :] = jnp.zeros((sb, POOL_HALO, pool_w), F32)
    e_ref[:, POOL_HALO:h2, :] = st_ref[...]
    e_ref[:, h2:, :] = xp
    sums = _pool_window_sums(e_ref, w2_ref, w4_ref, w8_ref, h2 + dec_seq)
    pos = PAST_LEN + lax.broadcasted_iota(jnp.int32, xp.shape, 1)
    delta = flat(_pool_delta(sums, xp, pos))
    y_pool = _dot(delta.astype(BF16), pbd_ref[...]) * pscale_ref[...]
    y_ref[:, 0:pool_w] = y_pool.astype(BF16)
    npool_ref[...] = e_ref[:, h2 + dec_seq - POOL_HALO:, :]

    lane = lax.broadcasted_iota(jnp.int32, (sb * kv_w, WINDOW), 1)
    keys_and_values = []
    for cache_ref, tok_ref, out_ref in ((ckt_ref, k_ref, nkt_ref), (cvt_ref, v_ref, nvt_ref)):
        old = cache_ref[...]
        tok_rows = jnp.concatenate([per_seq(tok_ref[...]), jnp.zeros((sb, WINDOW - dec_seq, kv_w), F32)], axis=1)
        new = jnp.swapaxes(tok_rows, 1, 2)
        shifted = jnp.where(lane >= WINDOW - dec_seq, pltpu.roll(flat(new), WINDOW - dec_seq, 1),
                            pltpu.roll(flat(old), WINDOW - dec_seq, 1))
        out_ref[...] = shifted.reshape(sb, kv_w, WINDOW)
        keys_and_values.append(jnp.concatenate([old, new], axis=2).astype(BF16))
    keys_t, values_t = keys_and_values

    n_keys = 2 * WINDOW
    n_rows = Q_PER_KV * dec_seq
    qi = lax.broadcasted_iota(jnp.int32, (n_rows, n_keys), 0) % dec_seq
    kj = lax.broadcasted_iota(jnp.int32, (n_rows, n_keys), 1)
    band = ((kj > qi) & (kj <= qi + WINDOW))[None]
    q = per_seq(q_ref[...])
    outs = []
    for g in range(N_KV_HEADS):
        heads = range(g * Q_PER_KV, (g + 1) * Q_PER_KV)
        rs = slice(g * HEAD_DIM, (g + 1) * HEAD_DIM)
        qg = jnp.concatenate([q[:, :, h * HEAD_DIM:(h + 1) * HEAD_DIM] for h in heads], axis=1)
        s = _bdot(qg.astype(BF16), keys_t[:, rs, :])
        sink = _per_head([sink_ref[layer, h] for h in heads], dec_seq, axis=0)[None]
        p = _softmax(jnp.where(band, s, -jnp.inf), sink)
        o = _bdot_nt(p.astype(BF16), values_t[:, rs, :])
        outs.extend(o[:, i * dec_seq:(i + 1) * dec_seq, :] for i in range(Q_PER_KV))
    y_ref[:, pool_w:pool_w + swa_w] = flat(jnp.concatenate(outs, axis=2)).astype(BF16)

    xq = per_seq(xq_ref[...])
    outs = []
    for h in range(N_X_HEADS):
        rs = slice(h * HEAD_DIM, (h + 1) * HEAD_DIM)
        p = _softmax(_bdot(xq[:, :, rs].astype(BF16), cmkt_ref[:, rs, :].astype(BF16)))
        outs.append(_bdot_nt(p.astype(BF16), cmvt_ref[:, rs, :].astype(BF16)))
    y_ref[:, pool_w + swa_w:] = flat(jnp.concatenate(outs, axis=2)).astype(BF16)


def _mix_sample(sink, xp, q, k, v, xq, cache_kt, cache_vt, state16, cache_mkt, cache_mvt, pool_bd, pool_scale,
                layer, n_prompt, dec_seq):
    dec_batch = cache_kt.shape[1]
    sb = SAMPLE_SEQS
    rows = sb * dec_seq
    off = n_prompt // rows
    pool_w, swa_w, kv_w, x_w = xp.shape[1], q.shape[1], k.shape[1], xq.shape[1]
    n_mem = cache_mkt.shape[3]
    d_out = pool_w + swa_w + x_w

    tok = lambda w: pl.BlockSpec((rows, w), lambda i: (off + i, 0))
    per_seq = lambda a, w: pl.BlockSpec((None, sb, a, w), lambda i: (layer, i, 0, 0))
    out_seq = lambda a, w: pl.BlockSpec((sb, a, w), lambda i: (i, 0, 0))
    e_shape = (sb, 2 * POOL_HALO + dec_seq, pool_w)
    return pl.pallas_call(
        functools.partial(_mix_sample_kernel, layer=layer, dec_seq=dec_seq),
        grid=(dec_batch // sb,),
        in_specs=[
            pl.BlockSpec(memory_space=pltpu.SMEM),
            tok(pool_w), tok(swa_w), tok(kv_w), tok(kv_w), tok(x_w),
            per_seq(kv_w, WINDOW), per_seq(kv_w, WINDOW), per_seq(POOL_HALO, pool_w),
            per_seq(x_w, n_mem), per_seq(x_w, n_mem),
            _resident((None, pool_w, pool_w), lambda i: (layer, 0, 0)),
            _resident((None, 1, pool_w), lambda i: (layer, 0, 0)),
        ],
        out_specs=(
            pl.BlockSpec((rows, d_out), lambda i: (i, 0)),
            out_seq(kv_w, WINDOW), out_seq(kv_w, WINDOW), out_seq(POOL_HALO, pool_w),
        ),
        out_shape=(
            jax.ShapeDtypeStruct((dec_batch * dec_seq, d_out), BF16),
            jax.ShapeDtypeStruct((dec_batch, kv_w, WINDOW), F32),
            jax.ShapeDtypeStruct((dec_batch, kv_w, WINDOW), F32),
            jax.ShapeDtypeStruct((dec_batch, POOL_HALO, pool_w), F32),
        ),
        scratch_shapes=[
            pltpu.VMEM(e_shape, F32), pltpu.VMEM(e_shape, F32), pltpu.VMEM(e_shape, F32), pltpu.VMEM(e_shape, F32),
        ],
        compiler_params=_cparams("parallel"),
        name="mix_sample",
    )(sink, xp, q, k, v, xq, cache_kt, cache_vt, state16, cache_mkt, cache_mvt, pool_bd, pool_scale)


def _merge_ln_kernel(x_ref, yp_ref, ys_ref, win_ref, wbp_ref, wbs_ref, wbx_ref, wo_ref, g_ref, b_ref, *refs,
                     alpha, n_first_tiles, gate_col0):
    n_cast = (len(refs) - 1) // 2
    o_ref = refs[n_cast]
    _cast_blocks(refs[:n_cast], refs[n_cast + 1:])
    d = x_ref.shape[1]
    in_first = pl.program_id(0) < n_first_tiles

    def finish(rows, z):
        o_ref[rows, :] = _layer_norm(z, g_ref[...], b_ref[...])

    pending = None
    for r0 in range(0, x_ref.shape[0], SUB_TILE):
        rows = slice(r0, r0 + SUB_TILE)
        x = x_ref[rows, :]
        xb = x.astype(BF16)
        y = jnp.where(in_first, yp_ref[rows, :], ys_ref[rows, :])
        c0 = 0
        merged = None
        for b, w_ref in enumerate((wbp_ref, wbs_ref, wbx_ref)):
            w = w_ref.shape[0]
            gate = jax.nn.sigmoid(_dot(xb, win_ref[:, gate_col0 + b * d:gate_col0 + (b + 1) * d]))
            term = gate * _dot(y[:, c0:c0 + w], w_ref[...])
            merged = term if merged is None else merged + term
            c0 += w
            if b == 0 and pending is not None:
                finish(*pending)
                pending = None
        pending = (rows, alpha * x + _dot(merged.astype(BF16), wo_ref[...]))
    finish(*pending)


def _merge_ln(x, y_p, y_s, w_in, w_br_pool, w_br_swa, w_br_cross, w_o, ln_g, ln_b, layer, alpha, gate_col0,
              cast_weights):
    n, d = x.shape
    tm = WIDE_TOKEN_TILE
    n_first_tiles = y_p.shape[0] // tm
    first, second = _split_rows(n_first_tiles)
    lsel = lambda i: (layer, 0, 0)
    res = lambda w: _resident((None,) + w.shape[1:], lsel)
    cast_in, cast_out, cast_shape = _cast_specs(cast_weights, layer, lambda i: jnp.minimum(i, N_CAST_BLOCKS - 1))
    assert n // tm >= N_CAST_BLOCKS
    out = pl.pallas_call(
        functools.partial(_merge_ln_kernel, alpha=alpha, n_first_tiles=n_first_tiles, gate_col0=gate_col0),
        grid=(n // tm,),
        in_specs=[
            pl.BlockSpec((tm, d), lambda i: (i, 0)),
            pl.BlockSpec((tm, y_p.shape[1]), first),
            pl.BlockSpec((tm, y_s.shape[1]), second),
            res(w_in), res(w_br_pool), res(w_br_swa), res(w_br_cross), res(w_o), res(ln_g), res(ln_b),
        ] + cast_in,
        out_specs=[pl.BlockSpec((tm, d), lambda i: (i, 0))] + cast_out,
        out_shape=[jax.ShapeDtypeStruct((n, d), F32)] + cast_shape,
        compiler_params=_cparams("arbitrary"),
        name="merge_ln",
    )(x, y_p, y_s, w_in, w_br_pool, w_br_swa, w_br_cross, w_o, ln_g, ln_b, *cast_weights)
    return out[0], tuple(out[1:])


def kernel(x_prompt, x_sample, cache_swa_k, cache_swa_v, state_pool, cache_mem_k, cache_mem_v, mem_prompt, w_mem_k, w_mem_v, ffn1_w_gate, ffn1_w_up, ffn1_w_down, ln1_g, ln1_b, w_in, pool_mix, pool_scale, attn_sink, w_br_pool, w_br_swa, w_br_cross, w_o, ln2_g, ln2_b, ffn2_w_gate, ffn2_w_up, ffn2_w_down, ln3_g, ln3_b):
    batch, seq, d = x_prompt.shape
    dec_batch, dec_seq, _ = x_sample.shape
    depth = w_in.shape[0]
    n_groups, pool_group = pool_mix.shape[1], pool_mix.shape[2]
    pool_w = n_groups * pool_group
    swa_w = w_br_swa.shape[1]
    kv_w = cache_swa_k.shape[3] * cache_swa_k.shape[4]
    x_w = w_br_cross.shape[1]
    n_mem = mem_prompt.shape[1]
    widths = (pool_w, swa_w, kv_w, x_w)
    proj_w = sum(widths) + kv_w
    assert pool_group == POOL_GROUP and swa_w == N_Q_HEADS * HEAD_DIM and kv_w == N_KV_HEADS * HEAD_DIM
    assert x_w == N_X_HEADS * HEAD_DIM and cache_swa_k.shape[2] == WINDOW
    assert state_pool.shape[2] == POOL_HALO - 1 and w_in.shape[2] == proj_w + 3 * d
    alpha = (2 * depth) ** 0.25
    n_prompt, n_sample = batch * seq, dec_batch * dec_seq

    bf = lambda w: w.astype(BF16)
    vec = lambda p: p.reshape(depth, 1, p.shape[-1])
    w_in_b = bf(w_in)
    pool_bd = bf(jnp.einsum("lgcd,gh->lgchd", pool_mix, jnp.eye(n_groups, dtype=F32))
                 .reshape(depth, pool_w, pool_w))
    state16 = jnp.pad(state_pool, ((0, 0), (0, 0), (1, 0), (0, 0)))
    to_cols = lambda c: c.transpose(0, 1, 3, 4, 2).reshape(depth, dec_batch, c.shape[3] * HEAD_DIM, c.shape[2])
    from_cols = lambda c, n_h: c.reshape(depth, dec_batch, n_h, HEAD_DIM, c.shape[-1]).transpose(0, 1, 4, 2, 3)
    cache_kt, cache_vt = to_cols(cache_swa_k), to_cols(cache_swa_v)
    cache_mkt, cache_mvt = to_cols(cache_mem_k), to_cols(cache_mem_v)

    tables = _rope_tables(seq, n_sample, dec_seq)
    mk, mv, mkh, mvt = _mem_kv(bf(mem_prompt.reshape(batch * n_mem, d)), bf(w_mem_k), bf(w_mem_v), batch)

    ffn1_f32 = (ffn1_w_gate, ffn1_w_up, ffn1_w_down)
    ffn2_f32 = (ffn2_w_gate, ffn2_w_up, ffn2_w_down)
    ffn1_w = tuple(bf(w[0]) for w in ffn1_f32)
    ffn1_ln = (vec(ln1_g), vec(ln1_b))
    ffn2_ln = (vec(ln3_g), vec(ln3_b))
    merge = (w_in_b, bf(w_br_pool), bf(w_br_swa), bf(w_br_cross), bf(w_o), vec(ln2_g), vec(ln2_b))
    pscale = vec(pool_scale)

    xs = (x_prompt.reshape(n_prompt, d), x_sample.reshape(n_sample, d))
    pk, pv, pp, sk, sv, sp = [], [], [], [], [], []
    for l in range(depth):
        x, xp, q, k, v, xq = _ffn(xs, *ffn1_w, *ffn1_ln, l, alpha, n_prompt, proj=(w_in_b, tables, seq, widths))
        next_ffn1 = ffn1_f32 if l + 1 < depth else ()
        y_p, nk, nv, npool_p, ffn1_w = _mix_prompt(attn_sink, xp, q, k, v, xq, mkh, mvt, pool_bd, pscale, l,
                                                   batch, seq, cast_weights=next_ffn1, cast_layer=l + 1)
        y_s, nkt, nvt, npool_s = _mix_sample(attn_sink, xp, q, k, v, xq, cache_kt, cache_vt, state16,
                                             cache_mkt, cache_mvt, pool_bd, pscale, l, n_prompt, dec_seq)
        x, ffn2_w = _merge_ln(x, y_p, y_s, *merge, l, alpha, proj_w, cast_weights=ffn2_f32)
        xs = _ffn((x,), *ffn2_w, *ffn2_ln, l, alpha, n_prompt, split_out=(l == depth - 1))
        pk.append(nk)
        pv.append(nv)
        pp.append(npool_p[:, 1:])
        sk.append(nkt)
        sv.append(nvt)
        sp.append(npool_s[:, 1:])

    heads = lambda a, n_h: a.reshape(a.shape[:-1] + (n_h, HEAD_DIM))
    return (xs[0].reshape(batch, seq, d), xs[1].reshape(dec_batch, dec_seq, d),
            heads(jnp.stack(pk), N_KV_HEADS), heads(jnp.stack(pv), N_KV_HEADS), jnp.stack(pp),
            heads(mk.reshape(depth, batch, n_mem, x_w), N_X_HEADS),
            heads(mv.reshape(depth, batch, n_mem, x_w), N_X_HEADS),
            from_cols(jnp.stack(sk), N_KV_HEADS), from_cols(jnp.stack(sv), N_KV_HEADS), jnp.stack(sp))
```

```python
import functools

import jax
import jax.numpy as jnp
from jax import lax
from jax.experimental import pallas as pl
from jax.experimental.pallas import tpu as pltpu

F32 = jnp.float32
BF16 = jnp.bfloat16

HEAD_DIM = 64
N_Q_HEADS = 8
N_KV_HEADS = 2
Q_PER_KV = N_Q_HEADS // N_KV_HEADS
N_X_HEADS = 4
WINDOW = 128
ROPE_THETA = 500000.0
ROT_DIM = 16
ROT_HALF = ROT_DIM // 2
POOL_GROUP = 64
POOL_HALO = 16
PAST_LEN = 8192
LN_EPS = 1e-5
SM_SCALE = HEAD_DIM ** -0.5

VMEM_LIMIT_BYTES_V7X = 56 * 1024 * 1024
LANES = 128

TOKEN_TILE = 512
WIDE_TOKEN_TILE = 1024
SUB_TILE = 256
FF_CHUNK = 256
SAMPLE_SEQS = 16


def _cparams(*semantics):
    return pltpu.CompilerParams(dimension_semantics=semantics, vmem_limit_bytes=VMEM_LIMIT_BYTES_V7X)


def _resident(shape, index_map):
    return pl.BlockSpec(shape, index_map, pipeline_mode=pl.Buffered(1))


def _dot(a, b):
    return jnp.dot(a, b, preferred_element_type=F32)


def _dot_nt(a, b):
    return lax.dot_general(a, b, (((1,), (1,)), ((), ())), preferred_element_type=F32)


def _bdot_nt(a, b):
    return lax.dot_general(a, b, (((2,), (2,)), ((0,), (0,))), preferred_element_type=F32)


def _bdot(a, b):
    return lax.dot_general(a, b, (((2,), (1,)), ((0,), (0,))), preferred_element_type=F32)


def _layer_norm(z, g, b):
    mu = jnp.mean(z, axis=-1, keepdims=True)
    zc = z - mu
    var = jnp.mean(zc * zc, axis=-1, keepdims=True)
    return zc * lax.rsqrt(var + LN_EPS) * g + b


N_CAST_BLOCKS = 16


def _cast_specs(weights, layers, block_of_step):
    in_specs, out_specs, out_shape = [], [], []
    for w, layer in zip(weights, layers):
        _, r, c = w.shape
        rb = r // N_CAST_BLOCKS
        assert rb * N_CAST_BLOCKS == r and rb % 16 == 0
        in_specs.append(pl.BlockSpec((None, rb, c), lambda *ids, layer=layer: (layer, block_of_step(*ids), 0)))
        out_specs.append(pl.BlockSpec((rb, c), lambda *ids: (block_of_step(*ids), 0)))
        out_shape.append(jax.ShapeDtypeStruct((r, c), BF16))
    return in_specs, out_specs, out_shape


def _cast_blocks(src_refs, dst_refs):
    for src, dst in zip(src_refs, dst_refs):
        dst[...] = src[...].astype(BF16)


def _split_rows(n_first_tiles):
    first = lambda i: (jnp.minimum(i, n_first_tiles - 1), 0)
    second = lambda i: (jnp.maximum(i - n_first_tiles, 0), 0)
    return first, second


def _rope_table_kernel(cos_ref, sp_ref, sm_ref, *, seq, dec_seq):
    n = cos_ref.shape[0]
    r = lax.broadcasted_iota(jnp.int32, (n, LANES), 0)
    lane = lax.broadcasted_iota(jnp.int32, (n, LANES), 1)
    pos = jnp.where(r < seq, r, PAST_LEN + (r - seq) % dec_seq)
    d = lane % HEAD_DIM
    fidx = (d % ROT_HALF).astype(F32)
    inv_freq = jnp.power(ROPE_THETA, -fidx * (2.0 / ROT_DIM))
    ang = pos.astype(F32) * inv_freq
    cos = jnp.cos(ang)
    sin = jnp.sin(ang)
    cos_ref[...] = jnp.where(d < ROT_DIM, cos, 1.0)
    sp_ref[...] = jnp.where((d >= ROT_HALF) & (d < ROT_DIM), sin, 0.0)
    sm_ref[...] = jnp.where(d < ROT_HALF, -sin, 0.0)


def _rope_tables(seq, n_sample_rows, dec_seq):
    n = seq + n_sample_rows
    out = jax.ShapeDtypeStruct((n, LANES), F32)
    return pl.pallas_call(
        functools.partial(_rope_table_kernel, seq=seq, dec_seq=dec_seq),
        out_shape=(out, out, out),
        name="rope_tables",
    )()


def _mem_kv_kernel(m_ref, wk_ref, wv_ref, k_ref, v_ref, kh_ref, vt_ref):
    m = m_ref[...]
    k = _dot(m, wk_ref[...])
    v = _dot(m, wv_ref[...])
    k_ref[...] = k
    v_ref[...] = v
    batch, n_heads, n_mem, _ = kh_ref.shape
    for b in range(batch):
        rows = slice(b * n_mem, (b + 1) * n_mem)
        for h in range(n_heads):
            kh_ref[b, h] = k[rows, h * HEAD_DIM:(h + 1) * HEAD_DIM].astype(BF16)
        vt_ref[b] = v[rows, :].T.astype(BF16)


def _mem_kv(mem, w_k, w_v, batch):
    depth, d, xw = w_k.shape
    rows = mem.shape[0]
    n_mem = rows // batch
    n_heads = xw // HEAD_DIM
    out = jax.ShapeDtypeStruct((depth, rows, xw), F32)
    w_spec = pl.BlockSpec((None, d, xw), lambda l: (l, 0, 0))
    o_spec = pl.BlockSpec((None, rows, xw), lambda l: (l, 0, 0))
    return pl.pallas_call(
        _mem_kv_kernel,
        grid=(depth,),
        in_specs=[pl.BlockSpec((rows, d), lambda l: (0, 0)), w_spec, w_spec],
        out_specs=(
            o_spec, o_spec,
            pl.BlockSpec((None, batch, n_heads, n_mem, HEAD_DIM), lambda l: (l, 0, 0, 0, 0)),
            pl.BlockSpec((None, batch, xw, n_mem), lambda l: (l, 0, 0, 0)),
        ),
        out_shape=(
            out, out,
            jax.ShapeDtypeStruct((depth, batch, n_heads, n_mem, HEAD_DIM), BF16),
            jax.ShapeDtypeStruct((depth, batch, xw, n_mem), BF16),
        ),
        compiler_params=_cparams("parallel"),
        name="mem_kv",
    )(mem, w_k, w_v)


def _ffn_kernel(*refs, alpha, n_first_tiles, split_in, split_out, proj_widths):
    refs = list(refs)
    x_refs = [refs.pop(0) for _ in range(2 if split_in else 1)]
    wg_ref, wu_ref, wd_ref, g_ref, b_ref = (refs.pop(0) for _ in range(5))
    if proj_widths:
        wp_ref, cos_ref, sp_ref, sm_ref = (refs.pop(0) for _ in range(4))
    o_refs = [refs.pop(0) for _ in range(2 if split_out else 1)]
    if proj_widths:
        xp_ref, q_ref, k_ref, v_ref, xq_ref = (refs.pop(0) for _ in range(5))
    (a_ref,) = refs
    i = pl.program_id(0)
    d_ff = wg_ref.shape[1]

    def finish(rows, z):
        out = _layer_norm(z, g_ref[...], b_ref[...])

        if split_out:
            @pl.when(i < n_first_tiles)
            def _():
                o_refs[0][rows, :] = out
            o_refs[1][rows, :] = out
        else:
            o_refs[0][rows, :] = out

        if proj_widths:
            pool_w, swa_w, kv_w, x_w = proj_widths
            ob = out.astype(BF16)
            cos, sp, sm = cos_ref[rows, :], sp_ref[rows, :], sm_ref[rows, :]

            def rope(t):
                return t * cos + pltpu.roll(t, ROT_HALF, 1) * sp + pltpu.roll(t, LANES - ROT_HALF, 1) * sm

            h = _dot(ob, wp_ref[...])
            c0 = 0
            xp_ref[rows, :] = h[:, c0:c0 + pool_w]
            c0 += pool_w
            for s in range(swa_w // LANES):
                q_ref[rows, s * LANES:(s + 1) * LANES] = rope(h[:, c0 + s * LANES:c0 + (s + 1) * LANES]) * SM_SCALE
            c0 += swa_w
            k_ref[rows, :] = rope(h[:, c0:c0 + kv_w])
            v_ref[rows, :] = h[:, c0 + kv_w:c0 + 2 * kv_w]
            c0 += 2 * kv_w
            xq_ref[rows, :] = h[:, c0:c0 + x_w] * SM_SCALE

    pending = None
    for r0 in range(0, a_ref.shape[0], SUB_TILE):
        rows = slice(r0, r0 + SUB_TILE)
        x = x_refs[0][rows, :]
        if split_in:
            x = jnp.where(i < n_first_tiles, x, x_refs[1][rows, :])
        xb = x.astype(BF16)
        for c in range(0, d_ff, FF_CHUNK):
            g = _dot(xb, wg_ref[:, c:c + FF_CHUNK])
            u = _dot(xb, wu_ref[:, c:c + FF_CHUNK])
            a_ref[rows, c:c + FF_CHUNK] = (g * jax.nn.sigmoid(g) * u).astype(BF16)
            if c == FF_CHUNK and pending is not None:
                finish(*pending)
                pending = None
        y = _dot(a_ref[rows, :], wd_ref[...])
        pending = (rows, alpha * x + 0.5 * y)
    finish(*pending)


def _ffn(xs,w_gate, w_up, w_down, ln_g, ln_b, layer, alpha, n_prompt, split_out=False, proj=None):
    split_in = len(xs) == 2
    d = xs[0].shape[1]
    n = sum(x.shape[0] for x in xs)
    d_ff = w_gate.shape[1]
    tm = TOKEN_TILE if proj is not None else WIDE_TOKEN_TILE
    n_first_tiles = n_prompt // tm
    first, second = _split_rows(n_first_tiles)
    whole = lambda i: (i, 0)
    lsel = lambda i: (layer, 0, 0)

    in_specs = [pl.BlockSpec((tm, d), first), pl.BlockSpec((tm, d), second)] if split_in else [pl.BlockSpec((tm, d), whole)]
    in_specs += [
        _resident((d, d_ff), lambda i: (0, 0)), _resident((d, d_ff), lambda i: (0, 0)),
        _resident((d_ff, d), lambda i: (0, 0)),
        _resident((None, 1, d), lsel), _resident((None, 1, d), lsel),
    ]
    args = list(xs) + [w_gate, w_up, w_down, ln_g, ln_b]
    if split_out:
        out_specs = [pl.BlockSpec((tm, d), first), pl.BlockSpec((tm, d), second)]
        out_shape = [jax.ShapeDtypeStruct((n_prompt, d), F32), jax.ShapeDtypeStruct((n - n_prompt, d), F32)]
    else:
        out_specs = [pl.BlockSpec((tm, d), whole)]
        out_shape = [jax.ShapeDtypeStruct((n, d), F32)]
    widths = None
    if proj is not None:
        w_in, tables, seq, widths = proj
        pool_w, swa_w, kv_w, x_w = widths
        assert kv_w == LANES
        seq_tiles = seq // tm

        def table_idx(i):
            return (jnp.where(i < n_first_tiles, i % seq_tiles, seq_tiles + i - n_first_tiles), 0)

        t_spec = pl.BlockSpec((tm, LANES), table_idx)
        in_specs += [_resident((d, sum(widths) + kv_w), lambda i: (0, 0)), t_spec, t_spec, t_spec]
        args += [w_in, *tables]
        for w in (pool_w, swa_w, kv_w, kv_w, x_w):
            out_specs.append(pl.BlockSpec((tm, w), whole))
            out_shape.append(jax.ShapeDtypeStruct((n, w), F32))
    return pl.pallas_call(
        functools.partial(_ffn_kernel, alpha=alpha, n_first_tiles=n_first_tiles, split_in=split_in,
                          split_out=split_out, proj_widths=widths),
        grid=(n // tm,),
        in_specs=in_specs,
        out_specs=out_specs,
        out_shape=out_shape,
        scratch_shapes=[pltpu.VMEM((tm, d_ff), BF16)],
        compiler_params=_cparams("arbitrary"),
        name="ffn_proj" if proj is not None else "ffn",
    )(*args)


def _pool_window_sums(e_ref, w2_ref, w4_ref, w8_ref, n):
    h = 2 * POOL_HALO
    lead = (slice(None),) * (len(e_ref.shape) - 2)
    rows = lambda a, b: lead + (slice(a, b), slice(None))
    sl = lambda ref, a, b: ref[rows(a, b)]
    w2_ref[rows(8, n)] = sl(e_ref, 8, n) + sl(e_ref, 7, n - 1)
    w4_ref[rows(16, n)] = sl(w2_ref, 16, n) + sl(w2_ref, 14, n - 2)
    w8_ref[rows(24, n)] = sl(w4_ref, 24, n) + sl(w4_ref, 20, n - 4)
    w16 = sl(w8_ref, h, n) + sl(w8_ref, h - 8, n - 8)
    return sl(w2_ref, h, n), sl(w4_ref, h, n), sl(w8_ref, h, n), w16


def _pool_delta(sums, xp, pos):
    w2, w4, w8, w16 = sums
    grp = lax.broadcasted_iota(jnp.int32, xp.shape, xp.ndim - 1) // POOL_GROUP
    s = jnp.where(grp == 0, w2, jnp.where(grp == 1, w4, jnp.where(grp == 2, w8, w16)))
    width = jnp.left_shift(2, grp)
    cnt = jnp.minimum(width, pos + 1).astype(F32)
    return s / cnt - xp


def _per_head(values, n_per_head, axis):
    n = len(values) * n_per_head
    shape = (n, 1) if axis == 0 else (1, n)
    head = lax.broadcasted_iota(jnp.int32, shape, axis) // n_per_head
    vec = values[-1]
    for i in reversed(range(len(values) - 1)):
        vec = jnp.where(head <= i, values[i], vec)
    return vec


def _softmax(s, sink=None, axis=-1):
    m = jnp.max(s, axis=axis, keepdims=True)
    if sink is not None:
        m = jnp.maximum(m, sink)
    p = jnp.exp(s - m)
    den = jnp.sum(p, axis=axis, keepdims=True)
    if sink is not None:
        den = den + jnp.exp(sink - m)
    return p * (1.0 / den)


def _mix_prompt_kernel(sink_ref, xp_ref, xph_ref, q_ref, k_ref, kh_ref, v_ref, vh_ref, xq_ref,
                       mkh_ref, mvt_ref, pbd_ref, pscale_ref, *refs, layer):
    n_cast = (len(refs) - 10) // 2
    y_ref, nk_ref, nv_ref, npool_ref = refs[n_cast:n_cast + 4]
    e_ref, w2_ref, w4_ref, w8_ref, kb_ref, vt_ref = refs[2 * n_cast + 4:]
    _cast_blocks(refs[:n_cast], refs[n_cast + 4:2 * n_cast + 4])
    t = pl.program_id(1)
    tq = xp_ref.shape[0]
    pool_w = xp_ref.shape[1]
    swa_w = q_ref.shape[1]

    nk_ref[...] = k_ref[tq - WINDOW:, :]
    nv_ref[...] = v_ref[tq - WINDOW:, :]
    npool_ref[...] = xp_ref[tq - POOL_HALO:, :]

    xp = xp_ref[...]
    e_ref[0:POOL_HALO, :] = jnp.zeros((POOL_HALO, pool_w), F32)
    e_ref[POOL_HALO:2 * POOL_HALO, :] = jnp.where(t == 0, 0.0, xph_ref[...])
    e_ref[2 * POOL_HALO:, :] = xp
    sums = _pool_window_sums(e_ref, w2_ref, w4_ref, w8_ref, tq + 2 * POOL_HALO)
    pos = t * tq + lax.broadcasted_iota(jnp.int32, (tq, pool_w), 0)
    delta = _pool_delta(sums, xp, pos)
    y_pool = _dot(delta.astype(BF16), pbd_ref[...]) * pscale_ref[...]
    y_ref[:, 0:pool_w] = y_pool.astype(BF16)

    def store_pair(top, bottom, rows, col0):
        y_ref[rows, col0:col0 + 2 * HEAD_DIM] = jnp.concatenate([top, bottom], axis=0).T.astype(BF16)

    for g in range(N_KV_HEADS):
        cs = slice(g * HEAD_DIM, (g + 1) * HEAD_DIM)
        kb_ref[g, 0:WINDOW, :] = kh_ref[:, cs].astype(BF16)
        kb_ref[g, WINDOW:, :] = k_ref[:, cs].astype(BF16)
    vt_ref[:, 0:WINDOW] = vh_ref[...].T.astype(BF16)
    vt_ref[:, WINDOW:] = v_ref[...].T.astype(BF16)
    n_cols = Q_PER_KV * WINDOW
    qi = lax.broadcasted_iota(jnp.int32, (WINDOW, n_cols), 1) % WINDOW
    from_prev = lax.broadcasted_iota(jnp.int32, (WINDOW, n_cols), 0) > qi
    for j in range(tq // WINDOW):
        rows = slice(j * WINDOW, (j + 1) * WINDOW)
        keys = slice(j * WINDOW, (j + 2) * WINDOW)
        for g in range(N_KV_HEADS):
            heads = range(g * Q_PER_KV, (g + 1) * Q_PER_KV)
            qg = jnp.concatenate([q_ref[rows, h * HEAD_DIM:(h + 1) * HEAD_DIM] for h in heads], axis=0)
            s = _dot_nt(kb_ref[g, keys, :], qg.astype(BF16))
            s_prev = s[:WINDOW, :]
            if j == 0:
                s_prev = jnp.where(t > 0, s_prev, -jnp.inf)
            sink = _per_head([sink_ref[layer, h] for h in heads], WINDOW, axis=1)
            p = _softmax(jnp.where(from_prev, s_prev, s[WINDOW:, :]), sink, axis=0)
            p2 = jnp.concatenate([jnp.where(from_prev, p, 0.0), jnp.where(from_prev, 0.0, p)], axis=0)
            o = _dot(vt_ref[g * HEAD_DIM:(g + 1) * HEAD_DIM, keys], p2.astype(BF16))
            for i in range(0, Q_PER_KV, 2):
                store_pair(o[:, i * WINDOW:(i + 1) * WINDOW], o[:, (i + 1) * WINDOW:(i + 2) * WINDOW],
                           rows, pool_w + (heads[0] + i) * HEAD_DIM)

    outs = []
    for h in range(N_X_HEADS):
        cs = slice(h * HEAD_DIM, (h + 1) * HEAD_DIM)
        p = _softmax(_dot_nt(mkh_ref[h], xq_ref[:, cs].astype(BF16)), axis=0)
        outs.append(_dot(mvt_ref[cs, :], p.astype(BF16)))
    for h in range(0, N_X_HEADS, 2):
        store_pair(outs[h], outs[h + 1], slice(None), pool_w + swa_w + h * HEAD_DIM)


def _mix_prompt(sink, xp, q, k, v, xq, mkh, mvt, pool_bd, pool_scale, layer, batch, seq,
                cast_weights=(), cast_layer=0):
    tq = WIDE_TOKEN_TILE
    n_t = seq // tq
    pool_w, swa_w, kv_w, x_w = xp.shape[1], q.shape[1], k.shape[1], xq.shape[1]
    n_mem = mvt.shape[3]
    d_out = pool_w + swa_w + x_w

    tile = lambda w: pl.BlockSpec((tq, w), lambda b, t: (b * n_t + t, 0))

    def halo(rows, w):
        per_tile = tq // rows
        return pl.BlockSpec((rows, w), lambda b, t: (jnp.maximum((b * n_t + t) * per_tile - 1, 0), 0))

    mem_k = pl.BlockSpec((None, None, N_X_HEADS, n_mem, HEAD_DIM), lambda b, t: (layer, b, 0, 0, 0))
    mem_v = pl.BlockSpec((None, None, x_w, n_mem), lambda b, t: (layer, b, 0, 0))
    steps_per_block = (batch * n_t) // N_CAST_BLOCKS
    assert steps_per_block * N_CAST_BLOCKS == batch * n_t
    cast_in, cast_out, cast_shape = _cast_specs(cast_weights, [cast_layer] * len(cast_weights),
                                                lambda b, t: (b * n_t + t) // steps_per_block)
    out = pl.pallas_call(
        functools.partial(_mix_prompt_kernel, layer=layer),
        grid=(batch, n_t),
        in_specs=[
            pl.BlockSpec(memory_space=pltpu.SMEM),
            tile(pool_w), halo(POOL_HALO, pool_w),
            tile(swa_w),
            tile(kv_w), halo(WINDOW, kv_w),
            tile(kv_w), halo(WINDOW, kv_w),
            tile(x_w),
            mem_k, mem_v,
            _resident((None, pool_w, pool_w), lambda b, t: (layer, 0, 0)),
            _resident((None, 1, pool_w), lambda b, t: (layer, 0, 0)),
        ] + cast_in,
        out_specs=[
            pl.BlockSpec((tq, d_out), lambda b, t: (b * n_t + t, 0)),
            pl.BlockSpec((None, WINDOW, kv_w), lambda b, t: (b, 0, 0)),
            pl.BlockSpec((None, WINDOW, kv_w), lambda b, t: (b, 0, 0)),
            pl.BlockSpec((None, POOL_HALO, pool_w), lambda b, t: (b, 0, 0)),
        ] + cast_out,
        out_shape=[
            jax.ShapeDtypeStruct((batch * seq, d_out), BF16),
            jax.ShapeDtypeStruct((batch, WINDOW, kv_w), F32),
            jax.ShapeDtypeStruct((batch, WINDOW, kv_w), F32),
            jax.ShapeDtypeStruct((batch, POOL_HALO, pool_w), F32),
        ] + cast_shape,
        scratch_shapes=[
            pltpu.VMEM((tq + 2 * POOL_HALO, pool_w), F32),
            pltpu.VMEM((tq + 2 * POOL_HALO, pool_w), F32),
            pltpu.VMEM((tq + 2 * POOL_HALO, pool_w), F32),
            pltpu.VMEM((tq + 2 * POOL_HALO, pool_w), F32),
            pltpu.VMEM((N_KV_HEADS, tq + WINDOW, HEAD_DIM), BF16),
            pltpu.VMEM((kv_w, tq + WINDOW), BF16),
        ],
        compiler_params=_cparams("arbitrary", "arbitrary"),
        name="mix_prompt",
    )(sink, xp, xp, q, k, k, v, v, xq, mkh, mvt, pool_bd, pool_scale, *cast_weights)
    return out[0], out[1], out[2], out[3], tuple(out[4:])


def _mix_sample_kernel(sink_ref, xp_ref, q_ref, k_ref, v_ref, xq_ref, ckt_ref, cvt_ref, st_ref,
                       cmkt_ref, cmvt_ref, pbd_ref, pscale_ref, *refs, layer, dec_seq):
    y_ref, nkt_ref, nvt_ref, npool_ref, e_ref, w2_ref, w4_ref, w8_ref = refs[-8:]
    sb = ckt_ref.shape[0]
    pool_w = xp_ref.shape[1]
    swa_w = q_ref.shape[1]
    kv_w = k_ref.shape[1]
    h2 = 2 * POOL_HALO
    per_seq = lambda a: a.reshape(sb, dec_seq, a.shape[-1])
    flat = lambda a: a.reshape(a.shape[0] * a.shape[1], a.shape[2])

    xp = per_seq(xp_ref[...])
    e_ref[:, 0:POOL_HALO, :] = jnp.zeros((sb, POOL_HALO, pool_w), F32)
    e_ref[:, POOL_HALO:h2, :] = st_ref[...]
    e_ref[:, h2:, :] = xp
    sums = _pool_window_sums(e_ref, w2_ref, w4_ref, w8_ref, h2 + dec_seq)
    pos = PAST_LEN + lax.broadcasted_iota(jnp.int32, xp.shape, 1)
    delta = flat(_pool_delta(sums, xp, pos))
    y_pool = _dot(delta.astype(BF16), pbd_ref[...]) * pscale_ref[...]
    y_ref[:, 0:pool_w] = y_pool.astype(BF16)
    npool_ref[...] = e_ref[:, h2 + dec_seq - POOL_HALO:, :]

    lane = lax.broadcasted_iota(jnp.int32, (sb * kv_w, WINDOW), 1)
    keys_and_values = []
    for cache_ref, tok_ref, out_ref in ((ckt_ref, k_ref, nkt_ref), (cvt_ref, v_ref, nvt_ref)):
        old = cache_ref[...]
        tok_rows = jnp.concatenate([per_seq(tok_ref[...]), jnp.zeros((sb, WINDOW - dec_seq, kv_w), F32)], axis=1)
        new = jnp.swapaxes(tok_rows, 1, 2)
        shifted = jnp.where(lane >= WINDOW - dec_seq, pltpu.roll(flat(new), WINDOW - dec_seq, 1),
                            pltpu.roll(flat(old), WINDOW - dec_seq, 1))
        out_ref[...] = shifted.reshape(sb, kv_w, WINDOW)
        keys_and_values.append(jnp.concatenate([old, new], axis=2).astype(BF16))
    keys_t, values_t = keys_and_values

    n_keys = 2 * WINDOW
    n_rows = Q_PER_KV * dec_seq
    qi = lax.broadcasted_iota(jnp.int32, (n_rows, n_keys), 0) % dec_seq
    kj = lax.broadcasted_iota(jnp.int32, (n_rows, n_keys), 1)
    band = ((kj > qi) & (kj <= qi + WINDOW))[None]
    q = per_seq(q_ref[...])
    outs = []
    for g in range(N_KV_HEADS):
        heads = range(g * Q_PER_KV, (g + 1) * Q_PER_KV)
        rs = slice(g * HEAD_DIM, (g + 1) * HEAD_DIM)
        qg = jnp.concatenate([q[:, :, h * HEAD_DIM:(h + 1) * HEAD_DIM] for h in heads], axis=1)
        s = _bdot(qg.astype(BF16), keys_t[:, rs, :])
        sink = _per_head([sink_ref[layer, h] for h in heads], dec_seq, axis=0)[None]
        p = _softmax(jnp.where(band, s, -jnp.inf), sink)
        o = _bdot_nt(p.astype(BF16), values_t[:, rs, :])
        outs.extend(o[:, i * dec_seq:(i + 1) * dec_seq, :] for i in range(Q_PER_KV))
    y_ref[:, pool_w:pool_w + swa_w] = flat(jnp.concatenate(outs, axis=2)).astype(BF16)

    xq = per_seq(xq_ref[...])
    outs = []
    for h in range(N_X_HEADS):
        rs = slice(h * HEAD_DIM, (h + 1) * HEAD_DIM)
        p = _softmax(_bdot(xq[:, :, rs].astype(BF16), cmkt_ref[:, rs, :].astype(BF16)))
        outs.append(_bdot_nt(p.astype(BF16), cmvt_ref[:, rs, :].astype(BF16)))
    y_ref[:, pool_w + swa_w:] = flat(jnp.concatenate(outs, axis=2)).astype(BF16)


def _mix_sample(sink, xp, q, k, v, xq, cache_kt, cache_vt, state16, cache_mkt, cache_mvt, pool_bd, pool_scale,
                layer, n_prompt, dec_seq, new_windows=()):
    depth, dec_batch = cache_kt.shape[:2]
    sb = SAMPLE_SEQS
    rows = sb * dec_seq
    off = n_prompt // rows
    pool_w, swa_w, kv_w, x_w = xp.shape[1], q.shape[1], k.shape[1], xq.shape[1]
    n_mem = cache_mkt.shape[3]
    d_out = pool_w + swa_w + x_w

    tok = lambda w: pl.BlockSpec((rows, w), lambda i: (off + i, 0))
    per_seq = lambda a, w: pl.BlockSpec((None, sb, a, w), lambda i: (layer, i, 0, 0))
    out_seq = lambda a, w: pl.BlockSpec((sb, a, w), lambda i: (i, 0, 0))
    e_shape = (sb, 2 * POOL_HALO + dec_seq, pool_w)
    n_fixed_inputs = 13
    return pl.pallas_call(
        functools.partial(_mix_sample_kernel, layer=layer, dec_seq=dec_seq),
        grid=(dec_batch // sb,),
        in_specs=[
            pl.BlockSpec(memory_space=pltpu.SMEM),
            tok(pool_w), tok(swa_w), tok(kv_w), tok(kv_w), tok(x_w),
            per_seq(kv_w, WINDOW), per_seq(kv_w, WINDOW), per_seq(POOL_HALO, pool_w),
            per_seq(x_w, n_mem), per_seq(x_w, n_mem),
            _resident((None, pool_w, pool_w), lambda i: (layer, 0, 0)),
            _resident((None, 1, pool_w), lambda i: (layer, 0, 0)),
        ] + [pl.BlockSpec(memory_space=pl.ANY)] * len(new_windows),
        out_specs=(
            pl.BlockSpec((rows, d_out), lambda i: (i, 0)),
            per_seq(kv_w, WINDOW), per_seq(kv_w, WINDOW), out_seq(POOL_HALO, pool_w),
        ),
        out_shape=(
            jax.ShapeDtypeStruct((dec_batch * dec_seq, d_out), BF16),
            jax.ShapeDtypeStruct((depth, dec_batch, kv_w, WINDOW), F32),
            jax.ShapeDtypeStruct((depth, dec_batch, kv_w, WINDOW), F32),
            jax.ShapeDtypeStruct((dec_batch, POOL_HALO, pool_w), F32),
        ),
        input_output_aliases={n_fixed_inputs + j: 1 + j for j in range(len(new_windows))},
        scratch_shapes=[
            pltpu.VMEM(e_shape, F32), pltpu.VMEM(e_shape, F32), pltpu.VMEM(e_shape, F32), pltpu.VMEM(e_shape, F32),
        ],
        compiler_params=_cparams("parallel"),
        name="mix_sample",
    )(sink, xp, q, k, v, xq, cache_kt, cache_vt, state16, cache_mkt, cache_mvt, pool_bd, pool_scale, *new_windows)


def _merge_ln_kernel(x_ref, yp_ref, ys_ref, win_ref, wbp_ref, wbs_ref, wbx_ref, wo_ref, g_ref, b_ref, *refs,
                     alpha, n_first_tiles, gate_col0):
    n_cast = (len(refs) - 1) // 2
    o_ref = refs[n_cast]
    _cast_blocks(refs[:n_cast], refs[n_cast + 1:])
    d = x_ref.shape[1]
    in_first = pl.program_id(0) < n_first_tiles

    def finish(rows, z):
        o_ref[rows, :] = _layer_norm(z, g_ref[...], b_ref[...])

    pending = None
    for r0 in range(0, x_ref.shape[0], SUB_TILE):
        rows = slice(r0, r0 + SUB_TILE)
        x = x_ref[rows, :]
        xb = x.astype(BF16)
        y = jnp.where(in_first, yp_ref[rows, :], ys_ref[rows, :])
        c0 = 0
        merged = None
        for b, w_ref in enumerate((wbp_ref, wbs_ref, wbx_ref)):
            w = w_ref.shape[0]
            gate = jax.nn.sigmoid(_dot(xb, win_ref[:, gate_col0 + b * d:gate_col0 + (b + 1) * d]))
            term = gate * _dot(y[:, c0:c0 + w], w_ref[...])
            merged = term if merged is None else merged + term
            c0 += w
            if b == 0 and pending is not None:
                finish(*pending)
                pending = None
        pending = (rows, alpha * x + _dot(merged.astype(BF16), wo_ref[...]))
    finish(*pending)


def _merge_ln(x, y_p, y_s, w_in, w_br_pool, w_br_swa, w_br_cross, w_o, ln_g, ln_b, layer, alpha, gate_col0,
              cast_weights, cast_layers):
    n, d = x.shape
    tm = WIDE_TOKEN_TILE
    n_first_tiles = y_p.shape[0] // tm
    first, second = _split_rows(n_first_tiles)
    lsel = lambda i: (layer, 0, 0)
    res = lambda w: _resident((None,) + w.shape[1:], lsel)
    cast_in, cast_out, cast_shape = _cast_specs(cast_weights, cast_layers,
                                                lambda i: jnp.minimum(i, N_CAST_BLOCKS - 1))
    assert n // tm >= N_CAST_BLOCKS
    out = pl.pallas_call(
        functools.partial(_merge_ln_kernel, alpha=alpha, n_first_tiles=n_first_tiles, gate_col0=gate_col0),
        grid=(n // tm,),
        in_specs=[
            pl.BlockSpec((tm, d), lambda i: (i, 0)),
            pl.BlockSpec((tm, y_p.shape[1]), first),
            pl.BlockSpec((tm, y_s.shape[1]), second),
            _resident(w_in.shape, lambda i: (0, 0)),
            res(w_br_pool), res(w_br_swa), res(w_br_cross), res(w_o), res(ln_g), res(ln_b),
        ] + cast_in,
        out_specs=[pl.BlockSpec((tm, d), lambda i: (i, 0))] + cast_out,
        out_shape=[jax.ShapeDtypeStruct((n, d), F32)] + cast_shape,
        compiler_params=_cparams("arbitrary"),
        name="merge_ln",
    )(x, y_p, y_s, w_in, w_br_pool, w_br_swa, w_br_cross, w_o, ln_g, ln_b, *cast_weights)
    return out[0], tuple(out[1:])


def kernel(x_prompt, x_sample, cache_swa_k, cache_swa_v, state_pool, cache_mem_k, cache_mem_v, mem_prompt, w_mem_k, w_mem_v, ffn1_w_gate, ffn1_w_up, ffn1_w_down, ln1_g, ln1_b, w_in, pool_mix, pool_scale, attn_sink, w_br_pool, w_br_swa, w_br_cross, w_o, ln2_g, ln2_b, ffn2_w_gate, ffn2_w_up, ffn2_w_down, ln3_g, ln3_b):
    batch, seq, d = x_prompt.shape
    dec_batch, dec_seq, _ = x_sample.shape
    depth = w_in.shape[0]
    n_groups, pool_group = pool_mix.shape[1], pool_mix.shape[2]
    pool_w = n_groups * pool_group
    swa_w = w_br_swa.shape[1]
    kv_w = cache_swa_k.shape[3] * cache_swa_k.shape[4]
    x_w = w_br_cross.shape[1]
    n_mem = mem_prompt.shape[1]
    widths = (pool_w, swa_w, kv_w, x_w)
    proj_w = sum(widths) + kv_w
    assert pool_group == POOL_GROUP and swa_w == N_Q_HEADS * HEAD_DIM and kv_w == N_KV_HEADS * HEAD_DIM
    assert x_w == N_X_HEADS * HEAD_DIM and cache_swa_k.shape[2] == WINDOW
    assert state_pool.shape[2] == POOL_HALO - 1 and w_in.shape[2] == proj_w + 3 * d
    alpha = (2 * depth) ** 0.25
    n_prompt, n_sample = batch * seq, dec_batch * dec_seq

    bf = lambda w: w.astype(BF16)
    vec = lambda p: p.reshape(depth, 1, p.shape[-1])
    pool_bd = bf(jnp.einsum("lgcd,gh->lgchd", pool_mix, jnp.eye(n_groups, dtype=F32))
                 .reshape(depth, pool_w, pool_w))
    state16 = jnp.pad(state_pool, ((0, 0), (0, 0), (1, 0), (0, 0)))
    to_cols = lambda c: c.transpose(0, 1, 3, 4, 2).reshape(depth, dec_batch, c.shape[3] * HEAD_DIM, c.shape[2])
    from_cols = lambda c, n_h: c.reshape(depth, dec_batch, n_h, HEAD_DIM, c.shape[-1]).transpose(0, 1, 4, 2, 3)
    cache_kt, cache_vt = to_cols(cache_swa_k), to_cols(cache_swa_v)
    cache_mkt, cache_mvt = to_cols(cache_mem_k), to_cols(cache_mem_v)

    tables = _rope_tables(seq, n_sample, dec_seq)
    mk, mv, mkh, mvt = _mem_kv(bf(mem_prompt.reshape(batch * n_mem, d)), bf(w_mem_k), bf(w_mem_v), batch)

    ffn1_f32 = (ffn1_w_gate, ffn1_w_up, ffn1_w_down)
    ffn2_f32 = (ffn2_w_gate, ffn2_w_up, ffn2_w_down)
    ffn1_w = tuple(bf(w[0]) for w in ffn1_f32)
    w_in_l = bf(w_in[0])
    ffn1_ln = (vec(ln1_g), vec(ln1_b))
    ffn2_ln = (vec(ln3_g), vec(ln3_b))
    merge = (bf(w_br_pool), bf(w_br_swa), bf(w_br_cross), bf(w_o), vec(ln2_g), vec(ln2_b))
    pscale = vec(pool_scale)

    xs = (x_prompt.reshape(n_prompt, d), x_sample.reshape(n_sample, d))
    pk, pv, pp, sp = [], [], [], []
    new_windows = ()
    for l in range(depth):
        more = l + 1 < depth
        x, xp, q, k, v, xq = _ffn(xs, *ffn1_w, *ffn1_ln, l, alpha, n_prompt, proj=(w_in_l, tables, seq, widths))
        y_p, nk, nv, npool_p, ffn1_w = _mix_prompt(attn_sink, xp, q, k, v, xq, mkh, mvt, pool_bd, pscale, l,
                                                   batch, seq, cast_weights=ffn1_f32 if more else (),
                                                   cast_layer=l + 1)
        y_s, *new_windows, npool_s = _mix_sample(attn_sink, xp, q, k, v, xq, cache_kt, cache_vt, state16,
                                                 cache_mkt, cache_mvt, pool_bd, pscale, l, n_prompt, dec_seq,
                                                 new_windows)
        x, casts = _merge_ln(x, y_p, y_s, w_in_l, *merge, l, alpha, proj_w,
                             cast_weights=ffn2_f32 + ((w_in,) if more else ()), cast_layers=(l, l, l, l + 1))
        w_in_l = casts[3] if more else None
        xs = _ffn((x,), *casts[:3], *ffn2_ln, l, alpha, n_prompt, split_out=not more)
        pk.append(nk)
        pv.append(nv)
        pp.append(npool_p[:, 1:])
        sp.append(npool_s[:, 1:])

    heads = lambda a, n_h: a.reshape(a.shape[:-1] + (n_h, HEAD_DIM))
    return (xs[0].reshape(batch, seq, d), xs[1].reshape(dec_batch, dec_seq, d),
            heads(jnp.stack(pk), N_KV_HEADS), heads(jnp.stack(pv), N_KV_HEADS), jnp.stack(pp),
            heads(mk.reshape(depth, batch, n_mem, x_w), N_X_HEADS),
            heads(mv.reshape(depth, batch, n_mem, x_w), N_X_HEADS),
            from_cols(new_windows[0], N_KV_HEADS), from_cols(new_windows[1], N_KV_HEADS), jnp.stack(sp))
```

```python
import functools

import jax
import jax.numpy as jnp
from jax import lax
from jax.experimental import pallas as pl
from jax.experimental.pallas import tpu as pltpu

F32 = jnp.float32
BF16 = jnp.bfloat16

HEAD_DIM = 64
N_Q_HEADS = 8
N_KV_HEADS = 2
Q_PER_KV = N_Q_HEADS // N_KV_HEADS
N_X_HEADS = 4
WINDOW = 128
ROPE_THETA = 500000.0
ROT_DIM = 16
ROT_HALF = ROT_DIM // 2
POOL_GROUP = 64
POOL_HALO = 16
PAST_LEN = 8192
LN_EPS = 1e-5
SM_SCALE = HEAD_DIM ** -0.5

VMEM_LIMIT_BYTES_V7X = 56 * 1024 * 1024
LANES = 128

TOKEN_TILE = 512
WIDE_TOKEN_TILE = 1024
SUB_TILE = 256
FF_CHUNK = 256
SAMPLE_SEQS = 16


def _cparams(*semantics):
    return pltpu.CompilerParams(dimension_semantics=semantics, vmem_limit_bytes=VMEM_LIMIT_BYTES_V7X)


def _resident(shape, index_map):
    return pl.BlockSpec(shape, index_map, pipeline_mode=pl.Buffered(1))


def _dot(a, b):
    return jnp.dot(a, b, preferred_element_type=F32)


def _dot_nt(a, b):
    return lax.dot_general(a, b, (((1,), (1,)), ((), ())), preferred_element_type=F32)


def _bdot_nt(a, b):
    return lax.dot_general(a, b, (((2,), (2,)), ((0,), (0,))), preferred_element_type=F32)


def _bdot(a, b):
    return lax.dot_general(a, b, (((2,), (1,)), ((0,), (0,))), preferred_element_type=F32)


def _layer_norm(z, g, b):
    mu = jnp.mean(z, axis=-1, keepdims=True)
    zc = z - mu
    var = jnp.mean(zc * zc, axis=-1, keepdims=True)
    return zc * lax.rsqrt(var + LN_EPS) * g + b


N_CAST_BLOCKS = 16


def _cast_specs(weights, layers, block_of_step):
    in_specs, out_specs, out_shape = [], [], []
    for w, layer in zip(weights, layers):
        _, r, c = w.shape
        rb = r // N_CAST_BLOCKS
        assert rb * N_CAST_BLOCKS == r and rb % 16 == 0
        in_specs.append(pl.BlockSpec((None, rb, c), lambda *ids, layer=layer: (layer, block_of_step(*ids), 0)))
        out_specs.append(pl.BlockSpec((rb, c), lambda *ids: (block_of_step(*ids), 0)))
        out_shape.append(jax.ShapeDtypeStruct((r, c), BF16))
    return in_specs, out_specs, out_shape


def _cast_blocks(src_refs, dst_refs):
    for src, dst in zip(src_refs, dst_refs):
        dst[...] = src[...].astype(BF16)


def _split_rows(n_first_tiles):
    first = lambda i: (jnp.minimum(i, n_first_tiles - 1), 0)
    second = lambda i: (jnp.maximum(i - n_first_tiles, 0), 0)
    return first, second


def _rope_table_kernel(cos_ref, sp_ref, sm_ref, *, seq, dec_seq):
    n = cos_ref.shape[0]
    r = lax.broadcasted_iota(jnp.int32, (n, LANES), 0)
    lane = lax.broadcasted_iota(jnp.int32, (n, LANES), 1)
    pos = jnp.where(r < seq, r, PAST_LEN + (r - seq) % dec_seq)
    d = lane % HEAD_DIM
    fidx = (d % ROT_HALF).astype(F32)
    inv_freq = jnp.power(ROPE_THETA, -fidx * (2.0 / ROT_DIM))
    ang = pos.astype(F32) * inv_freq
    cos = jnp.cos(ang)
    sin = jnp.sin(ang)
    cos_ref[...] = jnp.where(d < ROT_DIM, cos, 1.0)
    sp_ref[...] = jnp.where((d >= ROT_HALF) & (d < ROT_DIM), sin, 0.0)
    sm_ref[...] = jnp.where(d < ROT_HALF, -sin, 0.0)


def _rope_tables(seq, n_sample_rows, dec_seq):
    n = seq + n_sample_rows
    out = jax.ShapeDtypeStruct((n, LANES), F32)
    return pl.pallas_call(
        functools.partial(_rope_table_kernel, seq=seq, dec_seq=dec_seq),
        out_shape=(out, out, out),
        name="rope_tables",
    )()


def _mem_kv_kernel(m_ref, wk_ref, wv_ref, k_ref, v_ref, kh_ref, vt_ref):
    m = m_ref[...]
    k = _dot(m, wk_ref[...])
    v = _dot(m, wv_ref[...])
    k_ref[...] = k
    v_ref[...] = v
    batch, n_heads, n_mem, _ = kh_ref.shape
    for b in range(batch):
        rows = slice(b * n_mem, (b + 1) * n_mem)
        for h in range(n_heads):
            kh_ref[b, h] = k[rows, h * HEAD_DIM:(h + 1) * HEAD_DIM].astype(BF16)
        vt_ref[b] = v[rows, :].T.astype(BF16)


def _mem_kv(mem, w_k, w_v, batch):
    depth, d, xw = w_k.shape
    rows = mem.shape[0]
    n_mem = rows // batch
    n_heads = xw // HEAD_DIM
    out = jax.ShapeDtypeStruct((depth, rows, xw), F32)
    w_spec = pl.BlockSpec((None, d, xw), lambda l: (l, 0, 0))
    o_spec = pl.BlockSpec((None, rows, xw), lambda l: (l, 0, 0))
    return pl.pallas_call(
        _mem_kv_kernel,
        grid=(depth,),
        in_specs=[pl.BlockSpec((rows, d), lambda l: (0, 0)), w_spec, w_spec],
        out_specs=(
            o_spec, o_spec,
            pl.BlockSpec((None, batch, n_heads, n_mem, HEAD_DIM), lambda l: (l, 0, 0, 0, 0)),
            pl.BlockSpec((None, batch, xw, n_mem), lambda l: (l, 0, 0, 0)),
        ),
        out_shape=(
            out, out,
            jax.ShapeDtypeStruct((depth, batch, n_heads, n_mem, HEAD_DIM), BF16),
            jax.ShapeDtypeStruct((depth, batch, xw, n_mem), BF16),
        ),
        compiler_params=_cparams("parallel"),
        name="mem_kv",
    )(mem, w_k, w_v)


def _ffn_kernel(*refs, alpha, n_first_tiles, split_in, split_out, proj_widths):
    refs = list(refs)
    x_refs = [refs.pop(0) for _ in range(2 if split_in else 1)]
    wg_ref, wu_ref, wd_ref, g_ref, b_ref = (refs.pop(0) for _ in range(5))
    if proj_widths:
        wp_ref, cos_ref, sp_ref, sm_ref = (refs.pop(0) for _ in range(4))
    o_refs = [refs.pop(0) for _ in range(2 if split_out else 1)]
    if proj_widths:
        xp_ref, q_ref, k_ref, v_ref, xq_ref = (refs.pop(0) for _ in range(5))
    (a_ref,) = refs
    i = pl.program_id(0)
    d_ff = wg_ref.shape[1]

    def finish(rows, z):
        out = _layer_norm(z, g_ref[...], b_ref[...])

        if split_out:
            @pl.when(i < n_first_tiles)
            def _():
                o_refs[0][rows, :] = out
            o_refs[1][rows, :] = out
        else:
            o_refs[0][rows, :] = out

        if proj_widths:
            pool_w, swa_w, kv_w, x_w = proj_widths
            ob = out.astype(BF16)
            cos, sp, sm = cos_ref[rows, :], sp_ref[rows, :], sm_ref[rows, :]

            def rope(t):
                return t * cos + pltpu.roll(t, ROT_HALF, 1) * sp + pltpu.roll(t, LANES - ROT_HALF, 1) * sm

            h = _dot(ob, wp_ref[...])
            c0 = 0
            xp_ref[rows, :] = h[:, c0:c0 + pool_w]
            c0 += pool_w
            for s in range(swa_w // LANES):
                q_ref[rows, s * LANES:(s + 1) * LANES] = rope(h[:, c0 + s * LANES:c0 + (s + 1) * LANES]) * SM_SCALE
            c0 += swa_w
            k_ref[rows, :] = rope(h[:, c0:c0 + kv_w])
            v_ref[rows, :] = h[:, c0 + kv_w:c0 + 2 * kv_w]
            c0 += 2 * kv_w
            xq_ref[rows, :] = h[:, c0:c0 + x_w] * SM_SCALE

    pending = None
    for r0 in range(0, a_ref.shape[0], SUB_TILE):
        rows = slice(r0, r0 + SUB_TILE)
        x = x_refs[0][rows, :]
        if split_in:
            x = jnp.where(i < n_first_tiles, x, x_refs[1][rows, :])
        xb = x.astype(BF16)
        for c in range(0, d_ff, FF_CHUNK):
            g = _dot(xb, wg_ref[:, c:c + FF_CHUNK])
            u = _dot(xb, wu_ref[:, c:c + FF_CHUNK])
            a_ref[rows, c:c + FF_CHUNK] = (g * jax.nn.sigmoid(g) * u).astype(BF16)
            if c == FF_CHUNK and pending is not None:
                finish(*pending)
                pending = None
        y = _dot(a_ref[rows, :], wd_ref[...])
        pending = (rows, alpha * x + 0.5 * y)
    finish(*pending)


def _ffn(xs,w_gate, w_up, w_down, ln_g, ln_b, layer, alpha, n_prompt, split_out=False, proj=None):
    split_in = len(xs) == 2
    d = xs[0].shape[1]
    n = sum(x.shape[0] for x in xs)
    d_ff = w_gate.shape[1]
    tm = TOKEN_TILE if proj is not None else WIDE_TOKEN_TILE
    n_first_tiles = n_prompt // tm
    first, second = _split_rows(n_first_tiles)
    whole = lambda i: (i, 0)
    lsel = lambda i: (layer, 0, 0)

    in_specs = [pl.BlockSpec((tm, d), first), pl.BlockSpec((tm, d), second)] if split_in else [pl.BlockSpec((tm, d), whole)]
    in_specs += [
        _resident((d, d_ff), lambda i: (0, 0)), _resident((d, d_ff), lambda i: (0, 0)),
        _resident((d_ff, d), lambda i: (0, 0)),
        _resident((None, 1, d), lsel), _resident((None, 1, d), lsel),
    ]
    args = list(xs) + [w_gate, w_up, w_down, ln_g, ln_b]
    if split_out:
        out_specs = [pl.BlockSpec((tm, d), first), pl.BlockSpec((tm, d), second)]
        out_shape = [jax.ShapeDtypeStruct((n_prompt, d), F32), jax.ShapeDtypeStruct((n - n_prompt, d), F32)]
    else:
        out_specs = [pl.BlockSpec((tm, d), whole)]
        out_shape = [jax.ShapeDtypeStruct((n, d), F32)]
    widths = None
    if proj is not None:
        w_in, tables, seq, widths = proj
        pool_w, swa_w, kv_w, x_w = widths
        assert kv_w == LANES
        seq_tiles = seq // tm

        def table_idx(i):
            return (jnp.where(i < n_first_tiles, i % seq_tiles, seq_tiles + i - n_first_tiles), 0)

        t_spec = pl.BlockSpec((tm, LANES), table_idx)
        in_specs += [_resident((d, sum(widths) + kv_w), lambda i: (0, 0)), t_spec, t_spec, t_spec]
        args += [w_in, *tables]
        for w in (pool_w, swa_w, kv_w, kv_w, x_w):
            out_specs.append(pl.BlockSpec((tm, w), whole))
            out_shape.append(jax.ShapeDtypeStruct((n, w), F32))
    return pl.pallas_call(
        functools.partial(_ffn_kernel, alpha=alpha, n_first_tiles=n_first_tiles, split_in=split_in,
                          split_out=split_out, proj_widths=widths),
        grid=(n // tm,),
        in_specs=in_specs,
        out_specs=out_specs,
        out_shape=out_shape,
        scratch_shapes=[pltpu.VMEM((tm, d_ff), BF16)],
        compiler_params=_cparams("arbitrary"),
        name="ffn_proj" if proj is not None else "ffn",
    )(*args)


def _pool_window_sums(e_ref, w2_ref, w4_ref, w8_ref, n):
    h = 2 * POOL_HALO
    lead = (slice(None),) * (len(e_ref.shape) - 2)
    rows = lambda a, b: lead + (slice(a, b), slice(None))
    sl = lambda ref, a, b: ref[rows(a, b)]
    w2_ref[rows(8, n)] = sl(e_ref, 8, n) + sl(e_ref, 7, n - 1)
    w4_ref[rows(16, n)] = sl(w2_ref, 16, n) + sl(w2_ref, 14, n - 2)
    w8_ref[rows(24, n)] = sl(w4_ref, 24, n) + sl(w4_ref, 20, n - 4)
    w16 = sl(w8_ref, h, n) + sl(w8_ref, h - 8, n - 8)
    return sl(w2_ref, h, n), sl(w4_ref, h, n), sl(w8_ref, h, n), w16


def _pool_delta(sums, xp, pos):
    w2, w4, w8, w16 = sums
    grp = lax.broadcasted_iota(jnp.int32, xp.shape, xp.ndim - 1) // POOL_GROUP
    s = jnp.where(grp == 0, w2, jnp.where(grp == 1, w4, jnp.where(grp == 2, w8, w16)))
    width = jnp.left_shift(2, grp)
    cnt = jnp.minimum(width, pos + 1).astype(F32)
    return s / cnt - xp


def _per_head(values, n_per_head, axis):
    n = len(values) * n_per_head
    shape = (n, 1) if axis == 0 else (1, n)
    head = lax.broadcasted_iota(jnp.int32, shape, axis) // n_per_head
    vec = values[-1]
    for i in reversed(range(len(values) - 1)):
        vec = jnp.where(head <= i, values[i], vec)
    return vec


def _softmax(s, sink=None, axis=-1):
    m = jnp.max(s, axis=axis, keepdims=True)
    if sink is not None:
        m = jnp.maximum(m, sink)
    p = jnp.exp(s - m)
    den = jnp.sum(p, axis=axis, keepdims=True)
    if sink is not None:
        den = den + jnp.exp(sink - m)
    return p * (1.0 / den)


def _mix_prompt_kernel(sink_ref, xp_ref, xph_ref, q_ref, k_ref, kh_ref, v_ref, vh_ref, xq_ref,
                       mkh_ref, mvt_ref, pbd_ref, pscale_ref, *refs, layer):
    n_cast = (len(refs) - 10) // 2
    y_ref, nk_ref, nv_ref, npool_ref = refs[n_cast:n_cast + 4]
    e_ref, w2_ref, w4_ref, w8_ref, kb_ref, vt_ref = refs[2 * n_cast + 4:]
    _cast_blocks(refs[:n_cast], refs[n_cast + 4:2 * n_cast + 4])
    t = pl.program_id(1)
    tq = xp_ref.shape[0]
    pool_w = xp_ref.shape[1]
    swa_w = q_ref.shape[1]

    nk_ref[...] = k_ref[tq - WINDOW:, :]
    nv_ref[...] = v_ref[tq - WINDOW:, :]
    npool_ref[...] = xp_ref[tq - POOL_HALO:, :]

    xp = xp_ref[...]
    e_ref[0:POOL_HALO, :] = jnp.zeros((POOL_HALO, pool_w), F32)
    e_ref[POOL_HALO:2 * POOL_HALO, :] = jnp.where(t == 0, 0.0, xph_ref[...])
    e_ref[2 * POOL_HALO:, :] = xp
    sums = _pool_window_sums(e_ref, w2_ref, w4_ref, w8_ref, tq + 2 * POOL_HALO)
    pos = t * tq + lax.broadcasted_iota(jnp.int32, (tq, pool_w), 0)
    delta = _pool_delta(sums, xp, pos)
    y_pool = _dot(delta.astype(BF16), pbd_ref[...]) * pscale_ref[...]
    y_ref[:, 0:pool_w] = y_pool.astype(BF16)

    def store_pair(top, bottom, rows, col0):
        y_ref[rows, col0:col0 + 2 * HEAD_DIM] = jnp.concatenate([top, bottom], axis=0).T.astype(BF16)

    for g in range(N_KV_HEADS):
        cs = slice(g * HEAD_DIM, (g + 1) * HEAD_DIM)
        kb_ref[g, 0:WINDOW, :] = kh_ref[:, cs].astype(BF16)
        kb_ref[g, WINDOW:, :] = k_ref[:, cs].astype(BF16)
    vt_ref[:, 0:WINDOW] = vh_ref[...].T.astype(BF16)
    vt_ref[:, WINDOW:] = v_ref[...].T.astype(BF16)
    n_cols = Q_PER_KV * WINDOW
    qi = lax.broadcasted_iota(jnp.int32, (WINDOW, n_cols), 1) % WINDOW
    from_prev = lax.broadcasted_iota(jnp.int32, (WINDOW, n_cols), 0) > qi
    for j in range(tq // WINDOW):
        rows = slice(j * WINDOW, (j + 1) * WINDOW)
        keys = slice(j * WINDOW, (j + 2) * WINDOW)
        for g in range(N_KV_HEADS):
            heads = range(g * Q_PER_KV, (g + 1) * Q_PER_KV)
            qg = jnp.concatenate([q_ref[rows, h * HEAD_DIM:(h + 1) * HEAD_DIM] for h in heads], axis=0)
            s = _dot_nt(kb_ref[g, keys, :], qg.astype(BF16))
            s_prev = s[:WINDOW, :]
            if j == 0:
                s_prev = jnp.where(t > 0, s_prev, -jnp.inf)
            sink = _per_head([sink_ref[layer, h] for h in heads], WINDOW, axis=1)
            p = _softmax(jnp.where(from_prev, s_prev, s[WINDOW:, :]), sink, axis=0)
            p2 = jnp.concatenate([jnp.where(from_prev, p, 0.0), jnp.where(from_prev, 0.0, p)], axis=0)
            o = _dot(vt_ref[g * HEAD_DIM:(g + 1) * HEAD_DIM, keys], p2.astype(BF16))
            for i in range(0, Q_PER_KV, 2):
                store_pair(o[:, i * WINDOW:(i + 1) * WINDOW], o[:, (i + 1) * WINDOW:(i + 2) * WINDOW],
                           rows, pool_w + (heads[0] + i) * HEAD_DIM)

    outs = []
    for h in range(N_X_HEADS):
        cs = slice(h * HEAD_DIM, (h + 1) * HEAD_DIM)
        p = _softmax(_dot_nt(mkh_ref[h], xq_ref[:, cs].astype(BF16)), axis=0)
        outs.append(_dot(mvt_ref[cs, :], p.astype(BF16)))
    for h in range(0, N_X_HEADS, 2):
        store_pair(outs[h], outs[h + 1], slice(None), pool_w + swa_w + h * HEAD_DIM)


def _mix_prompt(sink, xp, q, k, v, xq, mkh, mvt, pool_bd, pool_scale, layer, batch, seq,
                cast_weights=(), cast_layer=0):
    tq = WIDE_TOKEN_TILE
    n_t = seq // tq
    pool_w, swa_w, kv_w, x_w = xp.shape[1], q.shape[1], k.shape[1], xq.shape[1]
    n_mem = mvt.shape[3]
    d_out = pool_w + swa_w + x_w

    tile = lambda w: pl.BlockSpec((tq, w), lambda b, t: (b * n_t + t, 0))

    def halo(rows, w):
        per_tile = tq // rows
        return pl.BlockSpec((rows, w), lambda b, t: (jnp.maximum((b * n_t + t) * per_tile - 1, 0), 0))

    mem_k = pl.BlockSpec((None, None, N_X_HEADS, n_mem, HEAD_DIM), lambda b, t: (layer, b, 0, 0, 0))
    mem_v = pl.BlockSpec((None, None, x_w, n_mem), lambda b, t: (layer, b, 0, 0))
    steps_per_block = (batch * n_t) // N_CAST_BLOCKS
    assert steps_per_block * N_CAST_BLOCKS == batch * n_t
    cast_in, cast_out, cast_shape = _cast_specs(cast_weights, [cast_layer] * len(cast_weights),
                                                lambda b, t: (b * n_t + t) // steps_per_block)
    out = pl.pallas_call(
        functools.partial(_mix_prompt_kernel, layer=layer),
        grid=(batch, n_t),
        in_specs=[
            pl.BlockSpec(memory_space=pltpu.SMEM),
            tile(pool_w), halo(POOL_HALO, pool_w),
            tile(swa_w),
            tile(kv_w), halo(WINDOW, kv_w),
            tile(kv_w), halo(WINDOW, kv_w),
            tile(x_w),
            mem_k, mem_v,
            _resident((None, pool_w, pool_w), lambda b, t: (layer, 0, 0)),
            _resident((None, 1, pool_w), lambda b, t: (layer, 0, 0)),
        ] + cast_in,
        out_specs=[
            pl.BlockSpec((tq, d_out), lambda b, t: (b * n_t + t, 0)),
            pl.BlockSpec((None, WINDOW, kv_w), lambda b, t: (b, 0, 0)),
            pl.BlockSpec((None, WINDOW, kv_w), lambda b, t: (b, 0, 0)),
            pl.BlockSpec((None, POOL_HALO, pool_w), lambda b, t: (b, 0, 0)),
        ] + cast_out,
        out_shape=[
            jax.ShapeDtypeStruct((batch * seq, d_out), BF16),
            jax.ShapeDtypeStruct((batch, WINDOW, kv_w), F32),
            jax.ShapeDtypeStruct((batch, WINDOW, kv_w), F32),
            jax.ShapeDtypeStruct((batch, POOL_HALO, pool_w), F32),
        ] + cast_shape,
        scratch_shapes=[
            pltpu.VMEM((tq + 2 * POOL_HALO, pool_w), F32),
            pltpu.VMEM((tq + 2 * POOL_HALO, pool_w), F32),
            pltpu.VMEM((tq + 2 * POOL_HALO, pool_w), F32),
            pltpu.VMEM((tq + 2 * POOL_HALO, pool_w), F32),
            pltpu.VMEM((N_KV_HEADS, tq + WINDOW, HEAD_DIM), BF16),
            pltpu.VMEM((kv_w, tq + WINDOW), BF16),
        ],
        compiler_params=_cparams("arbitrary", "arbitrary"),
        name="mix_prompt",
    )(sink, xp, xp, q, k, k, v, v, xq, mkh, mvt, pool_bd, pool_scale, *cast_weights)
    return out[0], out[1], out[2], out[3], tuple(out[4:])


def _mix_sample_kernel(sink_ref, xp_ref, q_ref, k_ref, v_ref, xq_ref, ckt_ref, cvt_ref, st_ref,
                       cmkt_ref, cmvt_ref, pbd_ref, pscale_ref, *refs, layer, dec_seq):
    y_ref, nkt_ref, nvt_ref, npool_ref, e_ref, w2_ref, w4_ref, w8_ref = refs[-8:]
    sb = ckt_ref.shape[0]
    pool_w = xp_ref.shape[1]
    swa_w = q_ref.shape[1]
    kv_w = k_ref.shape[1]
    h2 = 2 * POOL_HALO
    per_seq = lambda a: a.reshape(sb, dec_seq, a.shape[-1])
    flat = lambda a: a.reshape(a.shape[0] * a.shape[1], a.shape[2])

    xp = per_seq(xp_ref[...])
    e_ref[:, 0:POOL_HALO, :] = jnp.zeros((sb, POOL_HALO, pool_w), F32)
    e_ref[:, POOL_HALO:h2, :] = st_ref[...]
    e_ref[:, h2:, :] = xp
    sums = _pool_window_sums(e_ref, w2_ref, w4_ref, w8_ref, h2 + dec_seq)
    pos = PAST_LEN + lax.broadcasted_iota(jnp.int32, xp.shape, 1)
    delta = flat(_pool_delta(sums, xp, pos))
    y_pool = _dot(delta.astype(BF16), pbd_ref[...]) * pscale_ref[...]
    y_ref[:, 0:pool_w] = y_pool.astype(BF16)
    npool_ref[...] = e_ref[:, h2 + dec_seq - POOL_HALO:, :]

    first_new = WINDOW - dec_seq
    lane = lax.broadcasted_iota(jnp.int32, (sb * kv_w, WINDOW), 1)
    keys_and_values = []
    for cache_ref, tok_ref, out_ref in ((ckt_ref, k_ref, nkt_ref), (cvt_ref, v_ref, nvt_ref)):
        old = cache_ref[...]
        tok_rows = jnp.concatenate([jnp.zeros((sb, first_new, kv_w), F32), per_seq(tok_ref[...])], axis=1)
        new = jnp.swapaxes(tok_rows, 1, 2)
        shifted = jnp.where(lane >= first_new, flat(new), pltpu.roll(flat(old), first_new, 1))
        out_ref[...] = shifted.reshape(sb, kv_w, WINDOW)
        keys_and_values.append(jnp.concatenate([old, new], axis=2).astype(BF16))
    keys_t, values_t = keys_and_values

    n_keys = 2 * WINDOW
    n_rows = N_Q_HEADS * dec_seq
    qi = lax.broadcasted_iota(jnp.int32, (n_rows, n_keys), 0) % dec_seq
    kj = lax.broadcasted_iota(jnp.int32, (n_rows, n_keys), 1)
    new0 = WINDOW + first_new
    band = (((kj < WINDOW) & (kj > qi)) | ((kj >= new0) & (kj - new0 <= qi)))[None]
    q = per_seq(q_ref[...])
    no_q = jnp.zeros((sb, dec_seq, HEAD_DIM), F32)
    head_rows = []
    for h in range(N_Q_HEADS):
        qh = q[:, :, h * HEAD_DIM:(h + 1) * HEAD_DIM]
        head_rows.append(jnp.concatenate([qh if g == h // Q_PER_KV else no_q for g in range(N_KV_HEADS)], axis=2))
    q_blocks = jnp.concatenate(head_rows, axis=1)
    s = _bdot(q_blocks.astype(BF16), keys_t)
    sink = _per_head([sink_ref[layer, h] for h in range(N_Q_HEADS)], dec_seq, axis=0)[None]
    p = _softmax(jnp.where(band, s, -jnp.inf), sink)
    o = _bdot_nt(p.astype(BF16), values_t)
    outs = []
    for h in range(N_Q_HEADS):
        g = h // Q_PER_KV
        outs.append(o[:, h * dec_seq:(h + 1) * dec_seq, g * HEAD_DIM:(g + 1) * HEAD_DIM])
    y_ref[:, pool_w:pool_w + swa_w] = flat(jnp.concatenate(outs, axis=2)).astype(BF16)

    xq = per_seq(xq_ref[...])
    x_w = xq.shape[2]
    n_xrows = N_X_HEADS * dec_seq
    row_head = lax.broadcasted_iota(jnp.int32, (n_xrows, x_w), 0) // dec_seq
    col_head = lax.broadcasted_iota(jnp.int32, (n_xrows, x_w), 1) // HEAD_DIM
    own = (row_head == col_head)[None]
    xq_blocks = jnp.where(own, jnp.concatenate([xq] * N_X_HEADS, axis=1), 0.0)
    p = _softmax(_bdot(xq_blocks.astype(BF16), cmkt_ref[...].astype(BF16)))
    o = jnp.where(own, _bdot_nt(p.astype(BF16), cmvt_ref[...].astype(BF16)), 0.0)
    y_x = o[:, 0:dec_seq, :]
    for h in range(1, N_X_HEADS):
        y_x = y_x + o[:, h * dec_seq:(h + 1) * dec_seq, :]
    y_ref[:, pool_w + swa_w:] = flat(y_x).astype(BF16)


def _mix_sample(sink, xp, q, k, v, xq, cache_kt, cache_vt, state16, cache_mkt, cache_mvt, pool_bd, pool_scale,
                layer, n_prompt, dec_seq, new_windows=()):
    depth, dec_batch = cache_kt.shape[:2]
    sb = SAMPLE_SEQS
    rows = sb * dec_seq
    off = n_prompt // rows
    pool_w, swa_w, kv_w, x_w = xp.shape[1], q.shape[1], k.shape[1], xq.shape[1]
    n_mem = cache_mkt.shape[3]
    d_out = pool_w + swa_w + x_w

    tok = lambda w: pl.BlockSpec((rows, w), lambda i: (off + i, 0))
    per_seq = lambda a, w: pl.BlockSpec((None, sb, a, w), lambda i: (layer, i, 0, 0))
    out_seq = lambda a, w: pl.BlockSpec((sb, a, w), lambda i: (i, 0, 0))
    e_shape = (sb, 2 * POOL_HALO + dec_seq, pool_w)
    n_fixed_inputs = 13
    return pl.pallas_call(
        functools.partial(_mix_sample_kernel, layer=layer, dec_seq=dec_seq),
        grid=(dec_batch // sb,),
        in_specs=[
            pl.BlockSpec(memory_space=pltpu.SMEM),
            tok(pool_w), tok(swa_w), tok(kv_w), tok(kv_w), tok(x_w),
            per_seq(kv_w, WINDOW), per_seq(kv_w, WINDOW), per_seq(POOL_HALO, pool_w),
            per_seq(x_w, n_mem), per_seq(x_w, n_mem),
            _resident((None, pool_w, pool_w), lambda i: (layer, 0, 0)),
            _resident((None, 1, pool_w), lambda i: (layer, 0, 0)),
        ] + [pl.BlockSpec(memory_space=pl.ANY)] * len(new_windows),
        out_specs=(
            pl.BlockSpec((rows, d_out), lambda i: (i, 0)),
            per_seq(kv_w, WINDOW), per_seq(kv_w, WINDOW), out_seq(POOL_HALO, pool_w),
        ),
        out_shape=(
            jax.ShapeDtypeStruct((dec_batch * dec_seq, d_out), BF16),
            jax.ShapeDtypeStruct((depth, dec_batch, kv_w, WINDOW), F32),
            jax.ShapeDtypeStruct((depth, dec_batch, kv_w, WINDOW), F32),
            jax.ShapeDtypeStruct((dec_batch, POOL_HALO, pool_w), F32),
        ),
        input_output_aliases={n_fixed_inputs + j: 1 + j for j in range(len(new_windows))},
        scratch_shapes=[
            pltpu.VMEM(e_shape, F32), pltpu.VMEM(e_shape, F32), pltpu.VMEM(e_shape, F32), pltpu.VMEM(e_shape, F32),
        ],
        compiler_params=_cparams("parallel"),
        name="mix_sample",
    )(sink, xp, q, k, v, xq, cache_kt, cache_vt, state16, cache_mkt, cache_mvt, pool_bd, pool_scale, *new_windows)


def _merge_ln_kernel(x_ref, yp_ref, ys_ref, win_ref, wbp_ref, wbs_ref, wbx_ref, wo_ref, g_ref, b_ref, *refs,
                     alpha, n_first_tiles, gate_col0):
    n_cast = (len(refs) - 1) // 2
    o_ref = refs[n_cast]
    _cast_blocks(refs[:n_cast], refs[n_cast + 1:])
    d = x_ref.shape[1]
    in_first = pl.program_id(0) < n_first_tiles

    def finish(rows, z):
        o_ref[rows, :] = _layer_norm(z, g_ref[...], b_ref[...])

    pending = None
    for r0 in range(0, x_ref.shape[0], SUB_TILE):
        rows = slice(r0, r0 + SUB_TILE)
        x = x_ref[rows, :]
        xb = x.astype(BF16)
        y = jnp.where(in_first, yp_ref[rows, :], ys_ref[rows, :])
        c0 = 0
        merged = None
        for b, w_ref in enumerate((wbp_ref, wbs_ref, wbx_ref)):
            w = w_ref.shape[0]
            gate = jax.nn.sigmoid(_dot(xb, win_ref[:, gate_col0 + b * d:gate_col0 + (b + 1) * d]))
            term = gate * _dot(y[:, c0:c0 + w], w_ref[...])
            merged = term if merged is None else merged + term
            c0 += w
            if b == 0 and pending is not None:
                finish(*pending)
                pending = None
        pending = (rows, alpha * x + _dot(merged.astype(BF16), wo_ref[...]))
    finish(*pending)


def _merge_ln(x, y_p, y_s, w_in, w_br_pool, w_br_swa, w_br_cross, w_o, ln_g, ln_b, layer, alpha, gate_col0,
              cast_weights, cast_layers):
    n, d = x.shape
    tm = WIDE_TOKEN_TILE
    n_first_tiles = y_p.shape[0] // tm
    first, second = _split_rows(n_first_tiles)
    lsel = lambda i: (layer, 0, 0)
    res = lambda w: _resident((None,) + w.shape[1:], lsel)
    cast_in, cast_out, cast_shape = _cast_specs(cast_weights, cast_layers,
                                                lambda i: jnp.minimum(i, N_CAST_BLOCKS - 1))
    assert n // tm >= N_CAST_BLOCKS
    out = pl.pallas_call(
        functools.partial(_merge_ln_kernel, alpha=alpha, n_first_tiles=n_first_tiles, gate_col0=gate_col0),
        grid=(n // tm,),
        in_specs=[
            pl.BlockSpec((tm, d), lambda i: (i, 0)),
            pl.BlockSpec((tm, y_p.shape[1]), first),
            pl.BlockSpec((tm, y_s.shape[1]), second),
            _resident(w_in.shape, lambda i: (0, 0)),
            res(w_br_pool), res(w_br_swa), res(w_br_cross), res(w_o), res(ln_g), res(ln_b),
        ] + cast_in,
        out_specs=[pl.BlockSpec((tm, d), lambda i: (i, 0))] + cast_out,
        out_shape=[jax.ShapeDtypeStruct((n, d), F32)] + cast_shape,
        compiler_params=_cparams("arbitrary"),
        name="merge_ln",
    )(x, y_p, y_s, w_in, w_br_pool, w_br_swa, w_br_cross, w_o, ln_g, ln_b, *cast_weights)
    return out[0], tuple(out[1:])


def kernel(x_prompt, x_sample, cache_swa_k, cache_swa_v, state_pool, cache_mem_k, cache_mem_v, mem_prompt, w_mem_k, w_mem_v, ffn1_w_gate, ffn1_w_up, ffn1_w_down, ln1_g, ln1_b, w_in, pool_mix, pool_scale, attn_sink, w_br_pool, w_br_swa, w_br_cross, w_o, ln2_g, ln2_b, ffn2_w_gate, ffn2_w_up, ffn2_w_down, ln3_g, ln3_b):
    batch, seq, d = x_prompt.shape
    dec_batch, dec_seq, _ = x_sample.shape
    depth = w_in.shape[0]
    n_groups, pool_group = pool_mix.shape[1], pool_mix.shape[2]
    pool_w = n_groups * pool_group
    swa_w = w_br_swa.shape[1]
    kv_w = cache_swa_k.shape[3] * cache_swa_k.shape[4]
    x_w = w_br_cross.shape[1]
    n_mem = mem_prompt.shape[1]
    widths = (pool_w, swa_w, kv_w, x_w)
    proj_w = sum(widths) + kv_w
    assert pool_group == POOL_GROUP and swa_w == N_Q_HEADS * HEAD_DIM and kv_w == N_KV_HEADS * HEAD_DIM
    assert x_w == N_X_HEADS * HEAD_DIM and cache_swa_k.shape[2] == WINDOW
    assert state_pool.shape[2] == POOL_HALO - 1 and w_in.shape[2] == proj_w + 3 * d
    alpha = (2 * depth) ** 0.25
    n_prompt, n_sample = batch * seq, dec_batch * dec_seq

    bf = lambda w: w.astype(BF16)
    vec = lambda p: p.reshape(depth, 1, p.shape[-1])
    pool_bd = bf(jnp.einsum("lgcd,gh->lgchd", pool_mix, jnp.eye(n_groups, dtype=F32))
                 .reshape(depth, pool_w, pool_w))
    state16 = jnp.pad(state_pool, ((0, 0), (0, 0), (1, 0), (0, 0)))
    to_cols = lambda c: c.transpose(0, 1, 3, 4, 2).reshape(depth, dec_batch, c.shape[3] * HEAD_DIM, c.shape[2])
    from_cols = lambda c, n_h: c.reshape(depth, dec_batch, n_h, HEAD_DIM, c.shape[-1]).transpose(0, 1, 4, 2, 3)
    cache_kt, cache_vt = to_cols(cache_swa_k), to_cols(cache_swa_v)
    cache_mkt, cache_mvt = to_cols(cache_mem_k), to_cols(cache_mem_v)

    tables = _rope_tables(seq, n_sample, dec_seq)
    mk, mv, mkh, mvt = _mem_kv(bf(mem_prompt.reshape(batch * n_mem, d)), bf(w_mem_k), bf(w_mem_v), batch)

    ffn1_f32 = (ffn1_w_gate, ffn1_w_up, ffn1_w_down)
    ffn2_f32 = (ffn2_w_gate, ffn2_w_up, ffn2_w_down)
    ffn1_w = tuple(bf(w[0]) for w in ffn1_f32)
    w_in_l = bf(w_in[0])
    ffn1_ln = (vec(ln1_g), vec(ln1_b))
    ffn2_ln = (vec(ln3_g), vec(ln3_b))
    merge = (bf(w_br_pool), bf(w_br_swa), bf(w_br_cross), bf(w_o), vec(ln2_g), vec(ln2_b))
    pscale = vec(pool_scale)

    xs = (x_prompt.reshape(n_prompt, d), x_sample.reshape(n_sample, d))
    pk, pv, pp, sp = [], [], [], []
    new_windows = ()
    for l in range(depth):
        more = l + 1 < depth
        x, xp, q, k, v, xq = _ffn(xs, *ffn1_w, *ffn1_ln, l, alpha, n_prompt, proj=(w_in_l, tables, seq, widths))
        y_p, nk, nv, npool_p, ffn1_w = _mix_prompt(attn_sink, xp, q, k, v, xq, mkh, mvt, pool_bd, pscale, l,
                                                   batch, seq, cast_weights=ffn1_f32 if more else (),
                                                   cast_layer=l + 1)
        y_s, *new_windows, npool_s = _mix_sample(attn_sink, xp, q, k, v, xq, cache_kt, cache_vt, state16,
                                                 cache_mkt, cache_mvt, pool_bd, pscale, l, n_prompt, dec_seq,
                                                 new_windows)
        x, casts = _merge_ln(x, y_p, y_s, w_in_l, *merge, l, alpha, proj_w,
                             cast_weights=ffn2_f32 + ((w_in,) if more else ()), cast_layers=(l, l, l, l + 1))
        w_in_l = casts[3] if more else None
        xs = _ffn((x,), *casts[:3], *ffn2_ln, l, alpha, n_prompt, split_out=not more)
        pk.append(nk)
        pv.append(nv)
        pp.append(npool_p[:, 1:])
        sp.append(npool_s[:, 1:])

    heads = lambda a, n_h: a.reshape(a.shape[:-1] + (n_h, HEAD_DIM))
    return (xs[0].reshape(batch, seq, d), xs[1].reshape(dec_batch, dec_seq, d),
            heads(jnp.stack(pk), N_KV_HEADS), heads(jnp.stack(pv), N_KV_HEADS), jnp.stack(pp),
            heads(mk.reshape(depth, batch, n_mem, x_w), N_X_HEADS),
            heads(mv.reshape(depth, batch, n_mem, x_w), N_X_HEADS),
            from_cols(new_windows[0], N_KV_HEADS), from_cols(new_windows[1], N_KV_HEADS), jnp.stack(sp))
```

```python
import functools

import jax
import jax.numpy as jnp
from jax import lax
from jax.experimental import pallas as pl
from jax.experimental.pallas import tpu as pltpu

F32 = jnp.float32
BF16 = jnp.bfloat16

HEAD_DIM = 64
N_Q_HEADS = 8
N_KV_HEADS = 2
Q_PER_KV = N_Q_HEADS // N_KV_HEADS
N_X_HEADS = 4
WINDOW = 128
ROPE_THETA = 500000.0
ROT_DIM = 16
ROT_HALF = ROT_DIM // 2
POOL_GROUP = 64
POOL_HALO = 16
PAST_LEN = 8192
LN_EPS = 1e-5
SM_SCALE = HEAD_DIM ** -0.5

VMEM_LIMIT_BYTES_V7X = 56 * 1024 * 1024
LANES = 128

TOKEN_TILE = 512
WIDE_TOKEN_TILE = 1024
SUB_TILE = 256
FF_CHUNK = 256
SAMPLE_SEQS = 16


def _cparams(*semantics):
    return pltpu.CompilerParams(dimension_semantics=semantics, vmem_limit_bytes=VMEM_LIMIT_BYTES_V7X)


def _resident(shape, index_map):
    return pl.BlockSpec(shape, index_map, pipeline_mode=pl.Buffered(1))


def _dot(a, b):
    return jnp.dot(a, b, preferred_element_type=F32)


def _dot_nt(a, b):
    return lax.dot_general(a, b, (((1,), (1,)), ((), ())), preferred_element_type=F32)


def _bdot_nt(a, b):
    return lax.dot_general(a, b, (((2,), (2,)), ((0,), (0,))), preferred_element_type=F32)


def _bdot(a, b):
    return lax.dot_general(a, b, (((2,), (1,)), ((0,), (0,))), preferred_element_type=F32)


def _layer_norm(z, g, b):
    mu = jnp.mean(z, axis=-1, keepdims=True)
    zc = z - mu
    var = jnp.mean(zc * zc, axis=-1, keepdims=True)
    return zc * lax.rsqrt(var + LN_EPS) * g + b


N_CAST_BLOCKS = 16


def _cast_specs(weights, layers, block_of_step):
    in_specs, out_specs, out_shape = [], [], []
    for w, layer in zip(weights, layers):
        _, r, c = w.shape
        rb = r // N_CAST_BLOCKS
        assert rb * N_CAST_BLOCKS == r and rb % 16 == 0
        in_specs.append(pl.BlockSpec((None, rb, c), lambda *ids, layer=layer: (layer, block_of_step(*ids), 0)))
        out_specs.append(pl.BlockSpec((rb, c), lambda *ids: (block_of_step(*ids), 0)))
        out_shape.append(jax.ShapeDtypeStruct((r, c), BF16))
    return in_specs, out_specs, out_shape


def _cast_blocks(src_refs, dst_refs):
    for src, dst in zip(src_refs, dst_refs):
        dst[...] = src[...].astype(BF16)


def _split_rows(n_first_tiles):
    first = lambda i: (jnp.minimum(i, n_first_tiles - 1), 0)
    second = lambda i: (jnp.maximum(i - n_first_tiles, 0), 0)
    return first, second


def _rope_table_kernel(cos_ref, sp_ref, sm_ref, *, seq, dec_seq):
    n = cos_ref.shape[0]
    r = lax.broadcasted_iota(jnp.int32, (n, LANES), 0)
    lane = lax.broadcasted_iota(jnp.int32, (n, LANES), 1)
    pos = jnp.where(r < seq, r, PAST_LEN + (r - seq) % dec_seq)
    d = lane % HEAD_DIM
    fidx = (d % ROT_HALF).astype(F32)
    inv_freq = jnp.power(ROPE_THETA, -fidx * (2.0 / ROT_DIM))
    ang = pos.astype(F32) * inv_freq
    cos = jnp.cos(ang)
    sin = jnp.sin(ang)
    cos_ref[...] = jnp.where(d < ROT_DIM, cos, 1.0)
    sp_ref[...] = jnp.where((d >= ROT_HALF) & (d < ROT_DIM), sin, 0.0)
    sm_ref[...] = jnp.where(d < ROT_HALF, -sin, 0.0)


def _rope_tables(seq, n_sample_rows, dec_seq):
    n = seq + n_sample_rows
    out = jax.ShapeDtypeStruct((n, LANES), F32)
    return pl.pallas_call(
        functools.partial(_rope_table_kernel, seq=seq, dec_seq=dec_seq),
        out_shape=(out, out, out),
        name="rope_tables",
    )()


def _mem_kv_kernel(m_ref, wk_ref, wv_ref, k_ref, v_ref, kh_ref, vt_ref):
    m = m_ref[...]
    k = _dot(m, wk_ref[...])
    v = _dot(m, wv_ref[...])
    k_ref[...] = k
    v_ref[...] = v
    batch, n_heads, n_mem, _ = kh_ref.shape
    for b in range(batch):
        rows = slice(b * n_mem, (b + 1) * n_mem)
        for h in range(n_heads):
            kh_ref[b, h] = k[rows, h * HEAD_DIM:(h + 1) * HEAD_DIM].astype(BF16)
        vt_ref[b] = v[rows, :].T.astype(BF16)


def _mem_kv(mem, w_k, w_v, batch):
    depth, d, xw = w_k.shape
    rows = mem.shape[0]
    n_mem = rows // batch
    n_heads = xw // HEAD_DIM
    out = jax.ShapeDtypeStruct((depth, rows, xw), F32)
    w_spec = pl.BlockSpec((None, d, xw), lambda l: (l, 0, 0))
    o_spec = pl.BlockSpec((None, rows, xw), lambda l: (l, 0, 0))
    return pl.pallas_call(
        _mem_kv_kernel,
        grid=(depth,),
        in_specs=[pl.BlockSpec((rows, d), lambda l: (0, 0)), w_spec, w_spec],
        out_specs=(
            o_spec, o_spec,
            pl.BlockSpec((None, batch, n_heads, n_mem, HEAD_DIM), lambda l: (l, 0, 0, 0, 0)),
            pl.BlockSpec((None, batch, xw, n_mem), lambda l: (l, 0, 0, 0)),
        ),
        out_shape=(
            out, out,
            jax.ShapeDtypeStruct((depth, batch, n_heads, n_mem, HEAD_DIM), BF16),
            jax.ShapeDtypeStruct((depth, batch, xw, n_mem), BF16),
        ),
        compiler_params=_cparams("parallel"),
        name="mem_kv",
    )(mem, w_k, w_v)


def _ffn_kernel(*refs, alpha, n_first_tiles, split_in, split_out, proj_widths):
    refs = list(refs)
    x_refs = [refs.pop(0) for _ in range(2 if split_in else 1)]
    wg_ref, wu_ref, wd_ref, g_ref, b_ref = (refs.pop(0) for _ in range(5))
    if proj_widths:
        wp_ref, cos_ref, sp_ref, sm_ref = (refs.pop(0) for _ in range(4))
    o_refs = [refs.pop(0) for _ in range(2 if split_out else 1)]
    if proj_widths:
        xp_ref, q_ref, k_ref, v_ref, xq_ref = (refs.pop(0) for _ in range(5))
    (a_ref,) = refs
    i = pl.program_id(0)
    d_ff = wg_ref.shape[1]

    def finish(rows, z):
        out = _layer_norm(z, g_ref[...], b_ref[...])

        if split_out:
            @pl.when(i < n_first_tiles)
            def _():
                o_refs[0][rows, :] = out
            o_refs[1][rows, :] = out
        else:
            o_refs[0][rows, :] = out

        if proj_widths:
            pool_w, swa_w, kv_w, x_w = proj_widths
            ob = out.astype(BF16)
            cos, sp, sm = cos_ref[rows, :], sp_ref[rows, :], sm_ref[rows, :]

            def rope(t):
                return t * cos + pltpu.roll(t, ROT_HALF, 1) * sp + pltpu.roll(t, LANES - ROT_HALF, 1) * sm

            h = _dot(ob, wp_ref[...])
            c0 = 0
            xp_ref[rows, :] = h[:, c0:c0 + pool_w]
            c0 += pool_w
            for s in range(swa_w // LANES):
                q_ref[rows, s * LANES:(s + 1) * LANES] = rope(h[:, c0 + s * LANES:c0 + (s + 1) * LANES]) * SM_SCALE
            c0 += swa_w
            k_ref[rows, :] = rope(h[:, c0:c0 + kv_w])
            v_ref[rows, :] = h[:, c0 + kv_w:c0 + 2 * kv_w]
            c0 += 2 * kv_w
            xq_ref[rows, :] = h[:, c0:c0 + x_w] * SM_SCALE

    pending = None
    for r0 in range(0, a_ref.shape[0], SUB_TILE):
        rows = slice(r0, r0 + SUB_TILE)
        x = x_refs[0][rows, :]
        if split_in:
            x = jnp.where(i < n_first_tiles, x, x_refs[1][rows, :])
        xb = x.astype(BF16)
        for c in range(0, d_ff, FF_CHUNK):
            g = _dot(xb, wg_ref[:, c:c + FF_CHUNK])
            u = _dot(xb, wu_ref[:, c:c + FF_CHUNK])
            a_ref[rows, c:c + FF_CHUNK] = (g * jax.nn.sigmoid(g) * u).astype(BF16)
            if c == FF_CHUNK and pending is not None:
                finish(*pending)
                pending = None
        y = _dot(a_ref[rows, :], wd_ref[...])
        pending = (rows, alpha * x + 0.5 * y)
    finish(*pending)


def _ffn(xs,w_gate, w_up, w_down, ln_g, ln_b, layer, alpha, n_prompt, split_out=False, proj=None):
    split_in = len(xs) == 2
    d = xs[0].shape[1]
    n = sum(x.shape[0] for x in xs)
    d_ff = w_gate.shape[1]
    tm = TOKEN_TILE if proj is not None else WIDE_TOKEN_TILE
    n_first_tiles = n_prompt // tm
    first, second = _split_rows(n_first_tiles)
    whole = lambda i: (i, 0)
    lsel = lambda i: (layer, 0, 0)

    in_specs = [pl.BlockSpec((tm, d), first), pl.BlockSpec((tm, d), second)] if split_in else [pl.BlockSpec((tm, d), whole)]
    in_specs += [
        _resident((d, d_ff), lambda i: (0, 0)), _resident((d, d_ff), lambda i: (0, 0)),
        _resident((d_ff, d), lambda i: (0, 0)),
        _resident((None, 1, d), lsel), _resident((None, 1, d), lsel),
    ]
    args = list(xs) + [w_gate, w_up, w_down, ln_g, ln_b]
    if split_out:
        out_specs = [pl.BlockSpec((tm, d), first), pl.BlockSpec((tm, d), second)]
        out_shape = [jax.ShapeDtypeStruct((n_prompt, d), F32), jax.ShapeDtypeStruct((n - n_prompt, d), F32)]
    else:
        out_specs = [pl.BlockSpec((tm, d), whole)]
        out_shape = [jax.ShapeDtypeStruct((n, d), F32)]
    widths = None
    if proj is not None:
        w_in, tables, seq, widths = proj
        pool_w, swa_w, kv_w, x_w = widths
        assert kv_w == LANES
        seq_tiles = seq // tm

        def table_idx(i):
            return (jnp.where(i < n_first_tiles, i % seq_tiles, seq_tiles + i - n_first_tiles), 0)

        t_spec = pl.BlockSpec((tm, LANES), table_idx)
        in_specs += [_resident((d, sum(widths) + kv_w), lambda i: (0, 0)), t_spec, t_spec, t_spec]
        args += [w_in, *tables]
        for w in (pool_w, swa_w, kv_w, kv_w, x_w):
            out_specs.append(pl.BlockSpec((tm, w), whole))
            out_shape.append(jax.ShapeDtypeStruct((n, w), F32))
    return pl.pallas_call(
        functools.partial(_ffn_kernel, alpha=alpha, n_first_tiles=n_first_tiles, split_in=split_in,
                          split_out=split_out, proj_widths=widths),
        grid=(n // tm,),
        in_specs=in_specs,
        out_specs=out_specs,
        out_shape=out_shape,
        scratch_shapes=[pltpu.VMEM((tm, d_ff), BF16)],
        compiler_params=_cparams("arbitrary"),
        name="ffn_proj" if proj is not None else "ffn",
    )(*args)


def _pool_window_sums(e_ref, w2_ref, w4_ref, w8_ref, n):
    h = 2 * POOL_HALO
    lead = (slice(None),) * (len(e_ref.shape) - 2)
    rows = lambda a, b: lead + (slice(a, b), slice(None))
    sl = lambda ref, a, b: ref[rows(a, b)]
    w2_ref[rows(8, n)] = sl(e_ref, 8, n) + sl(e_ref, 7, n - 1)
    w4_ref[rows(16, n)] = sl(w2_ref, 16, n) + sl(w2_ref, 14, n - 2)
    w8_ref[rows(24, n)] = sl(w4_ref, 24, n) + sl(w4_ref, 20, n - 4)
    w16 = sl(w8_ref, h, n) + sl(w8_ref, h - 8, n - 8)
    return sl(w2_ref, h, n), sl(w4_ref, h, n), sl(w8_ref, h, n), w16


def _pool_delta(sums, xp, pos):
    w2, w4, w8, w16 = sums
    grp = lax.broadcasted_iota(jnp.int32, xp.shape, xp.ndim - 1) // POOL_GROUP
    s = jnp.where(grp == 0, w2, jnp.where(grp == 1, w4, jnp.where(grp == 2, w8, w16)))
    width = jnp.left_shift(2, grp)
    cnt = jnp.minimum(width, pos + 1).astype(F32)
    return s / cnt - xp


def _per_head(values, n_per_head, axis):
    n = len(values) * n_per_head
    shape = (n, 1) if axis == 0 else (1, n)
    head = lax.broadcasted_iota(jnp.int32, shape, axis) // n_per_head
    vec = values[-1]
    for i in reversed(range(len(values) - 1)):
        vec = jnp.where(head <= i, values[i], vec)
    return vec


def _softmax(s, sink=None, axis=-1):
    m = jnp.max(s, axis=axis, keepdims=True)
    if sink is not None:
        m = jnp.maximum(m, sink)
    p = jnp.exp(s - m)
    den = jnp.sum(p, axis=axis, keepdims=True)
    if sink is not None:
        den = den + jnp.exp(sink - m)
    return p * (1.0 / den)


def _mix_prompt_kernel(sink_ref, xp_ref, xph_ref, q_ref, k_ref, kh_ref, v_ref, vh_ref, xq_ref,
                       mkh_ref, mvt_ref, pbd_ref, pscale_ref, *refs, layer):
    n_cast = (len(refs) - 10) // 2
    y_ref, nk_ref, nv_ref, npool_ref = refs[n_cast:n_cast + 4]
    e_ref, w2_ref, w4_ref, w8_ref, kb_ref, vt_ref = refs[2 * n_cast + 4:]
    _cast_blocks(refs[:n_cast], refs[n_cast + 4:2 * n_cast + 4])
    t = pl.program_id(1)
    tq = xp_ref.shape[0]
    pool_w = xp_ref.shape[1]
    swa_w = q_ref.shape[1]

    nk_ref[...] = k_ref[tq - WINDOW:, :]
    nv_ref[...] = v_ref[tq - WINDOW:, :]
    npool_ref[...] = xp_ref[tq - POOL_HALO:, :]

    xp = xp_ref[...]
    e_ref[0:POOL_HALO, :] = jnp.zeros((POOL_HALO, pool_w), F32)
    e_ref[POOL_HALO:2 * POOL_HALO, :] = jnp.where(t == 0, 0.0, xph_ref[...])
    e_ref[2 * POOL_HALO:, :] = xp
    sums = _pool_window_sums(e_ref, w2_ref, w4_ref, w8_ref, tq + 2 * POOL_HALO)
    pos = t * tq + lax.broadcasted_iota(jnp.int32, (tq, pool_w), 0)
    delta = _pool_delta(sums, xp, pos)
    y_pool = _dot(delta.astype(BF16), pbd_ref[...]) * pscale_ref[...]
    y_ref[:, 0:pool_w] = y_pool.astype(BF16)

    def store_pair(top, bottom, rows, col0):
        y_ref[rows, col0:col0 + 2 * HEAD_DIM] = jnp.concatenate([top, bottom], axis=0).T.astype(BF16)

    for g in range(N_KV_HEADS):
        cs = slice(g * HEAD_DIM, (g + 1) * HEAD_DIM)
        kb_ref[g, 0:WINDOW, :] = kh_ref[:, cs].astype(BF16)
        kb_ref[g, WINDOW:, :] = k_ref[:, cs].astype(BF16)
    vt_ref[:, 0:WINDOW] = vh_ref[...].T.astype(BF16)
    vt_ref[:, WINDOW:] = v_ref[...].T.astype(BF16)
    n_cols = Q_PER_KV * WINDOW
    qi = lax.broadcasted_iota(jnp.int32, (WINDOW, n_cols), 1) % WINDOW
    from_prev = lax.broadcasted_iota(jnp.int32, (WINDOW, n_cols), 0) > qi
    for j in range(tq // WINDOW):
        rows = slice(j * WINDOW, (j + 1) * WINDOW)
        keys = slice(j * WINDOW, (j + 2) * WINDOW)
        for g in range(N_KV_HEADS):
            heads = range(g * Q_PER_KV, (g + 1) * Q_PER_KV)
            qg = jnp.concatenate([q_ref[rows, h * HEAD_DIM:(h + 1) * HEAD_DIM] for h in heads], axis=0)
            s = _dot_nt(kb_ref[g, keys, :], qg.astype(BF16))
            s_prev = s[:WINDOW, :]
            if j == 0:
                s_prev = jnp.where(t > 0, s_prev, -jnp.inf)
            sink = _per_head([sink_ref[layer, h] for h in heads], WINDOW, axis=1)
            p = _softmax(jnp.where(from_prev, s_prev, s[WINDOW:, :]), sink, axis=0)
            p2 = jnp.concatenate([jnp.where(from_prev, p, 0.0), jnp.where(from_prev, 0.0, p)], axis=0)
            o = _dot(vt_ref[g * HEAD_DIM:(g + 1) * HEAD_DIM, keys], p2.astype(BF16))
            for i in range(0, Q_PER_KV, 2):
                store_pair(o[:, i * WINDOW:(i + 1) * WINDOW], o[:, (i + 1) * WINDOW:(i + 2) * WINDOW],
                           rows, pool_w + (heads[0] + i) * HEAD_DIM)

    outs = []
    for h in range(N_X_HEADS):
        cs = slice(h * HEAD_DIM, (h + 1) * HEAD_DIM)
        p = _softmax(_dot_nt(mkh_ref[h], xq_ref[:, cs].astype(BF16)), axis=0)
        outs.append(_dot(mvt_ref[cs, :], p.astype(BF16)))
    for h in range(0, N_X_HEADS, 2):
        store_pair(outs[h], outs[h + 1], slice(None), pool_w + swa_w + h * HEAD_DIM)


def _mix_prompt(sink, xp, q, k, v, xq, mkh, mvt, pool_bd, pool_scale, layer, batch, seq,
                cast_weights=(), cast_layer=0):
    tq = WIDE_TOKEN_TILE
    n_t = seq // tq
    pool_w, swa_w, kv_w, x_w = xp.shape[1], q.shape[1], k.shape[1], xq.shape[1]
    n_mem = mvt.shape[3]
    d_out = pool_w + swa_w + x_w

    tile = lambda w: pl.BlockSpec((tq, w), lambda b, t: (b * n_t + t, 0))

    def halo(rows, w):
        per_tile = tq // rows
        return pl.BlockSpec((rows, w), lambda b, t: (jnp.maximum((b * n_t + t) * per_tile - 1, 0), 0))

    mem_k = pl.BlockSpec((None, None, N_X_HEADS, n_mem, HEAD_DIM), lambda b, t: (layer, b, 0, 0, 0))
    mem_v = pl.BlockSpec((None, None, x_w, n_mem), lambda b, t: (layer, b, 0, 0))
    steps_per_block = (batch * n_t) // N_CAST_BLOCKS
    assert steps_per_block * N_CAST_BLOCKS == batch * n_t
    cast_in, cast_out, cast_shape = _cast_specs(cast_weights, [cast_layer] * len(cast_weights),
                                                lambda b, t: (b * n_t + t) // steps_per_block)
    out = pl.pallas_call(
        functools.partial(_mix_prompt_kernel, layer=layer),
        grid=(batch, n_t),
        in_specs=[
            pl.BlockSpec(memory_space=pltpu.SMEM),
            tile(pool_w), halo(POOL_HALO, pool_w),
            tile(swa_w),
            tile(kv_w), halo(WINDOW, kv_w),
            tile(kv_w), halo(WINDOW, kv_w),
            tile(x_w),
            mem_k, mem_v,
            _resident((None, pool_w, pool_w), lambda b, t: (layer, 0, 0)),
            _resident((None, 1, pool_w), lambda b, t: (layer, 0, 0)),
        ] + cast_in,
        out_specs=[
            pl.BlockSpec((tq, d_out), lambda b, t: (b * n_t + t, 0)),
            pl.BlockSpec((None, WINDOW, kv_w), lambda b, t: (b, 0, 0)),
            pl.BlockSpec((None, WINDOW, kv_w), lambda b, t: (b, 0, 0)),
            pl.BlockSpec((None, POOL_HALO, pool_w), lambda b, t: (b, 0, 0)),
        ] + cast_out,
        out_shape=[
            jax.ShapeDtypeStruct((batch * seq, d_out), BF16),
            jax.ShapeDtypeStruct((batch, WINDOW, kv_w), F32),
            jax.ShapeDtypeStruct((batch, WINDOW, kv_w), F32),
            jax.ShapeDtypeStruct((batch, POOL_HALO, pool_w), F32),
        ] + cast_shape,
        scratch_shapes=[
            pltpu.VMEM((tq + 2 * POOL_HALO, pool_w), F32),
            pltpu.VMEM((tq + 2 * POOL_HALO, pool_w), F32),
            pltpu.VMEM((tq + 2 * POOL_HALO, pool_w), F32),
            pltpu.VMEM((tq + 2 * POOL_HALO, pool_w), F32),
            pltpu.VMEM((N_KV_HEADS, tq + WINDOW, HEAD_DIM), BF16),
            pltpu.VMEM((kv_w, tq + WINDOW), BF16),
        ],
        compiler_params=_cparams("arbitrary", "arbitrary"),
        name="mix_prompt",
    )(sink, xp, xp, q, k, k, v, v, xq, mkh, mvt, pool_bd, pool_scale, *cast_weights)
    return out[0], out[1], out[2], out[3], tuple(out[4:])


def _mix_sample_kernel(sink_ref, xp_ref, q_ref, k_ref, v_ref, xq_ref, ckt_ref, cvt_ref, st_ref,
                       cmkt_ref, cmvt_ref, pbd_ref, pscale_ref, *refs, layer, dec_seq):
    y_ref, nkt_ref, nvt_ref, npool_ref, e_ref, w2_ref, w4_ref, w8_ref = refs[-8:]
    sb = ckt_ref.shape[0]
    pool_w = xp_ref.shape[1]
    swa_w = q_ref.shape[1]
    kv_w = k_ref.shape[1]
    h2 = 2 * POOL_HALO
    per_seq = lambda a: a.reshape(sb, dec_seq, a.shape[-1])
    flat = lambda a: a.reshape(a.shape[0] * a.shape[1], a.shape[2])

    xp = per_seq(xp_ref[...])
    e_ref[:, 0:POOL_HALO, :] = jnp.zeros((sb, POOL_HALO, pool_w), F32)
    e_ref[:, POOL_HALO:h2, :] = st_ref[...]
    e_ref[:, h2:, :] = xp
    sums = _pool_window_sums(e_ref, w2_ref, w4_ref, w8_ref, h2 + dec_seq)
    pos = PAST_LEN + lax.broadcasted_iota(jnp.int32, xp.shape, 1)
    delta = flat(_pool_delta(sums, xp, pos))
    y_pool = _dot(delta.astype(BF16), pbd_ref[...]) * pscale_ref[...]
    y_ref[:, 0:pool_w] = y_pool.astype(BF16)
    npool_ref[...] = e_ref[:, h2 + dec_seq - POOL_HALO:, :]

    first_new = WINDOW - dec_seq
    lane = lax.broadcasted_iota(jnp.int32, (sb * kv_w, WINDOW), 1)
    keys_and_values = []
    for cache_ref, tok_ref, out_ref in ((ckt_ref, k_ref, nkt_ref), (cvt_ref, v_ref, nvt_ref)):
        old = cache_ref[...]
        tok_rows = jnp.concatenate([jnp.zeros((sb, first_new, kv_w), F32), per_seq(tok_ref[...])], axis=1)
        new = jnp.swapaxes(tok_rows, 1, 2)
        shifted = jnp.where(lane >= first_new, flat(new), pltpu.roll(flat(old), first_new, 1))
        out_ref[...] = shifted.reshape(sb, kv_w, WINDOW)
        keys_and_values.append(jnp.concatenate([old, new], axis=2).astype(BF16))
    keys_t, values_t = keys_and_values

    n_keys = 2 * WINDOW
    n_rows = N_Q_HEADS * dec_seq
    qi = lax.broadcasted_iota(jnp.int32, (n_rows, n_keys), 0) % dec_seq
    kj = lax.broadcasted_iota(jnp.int32, (n_rows, n_keys), 1)
    new0 = WINDOW + first_new
    band = (((kj < WINDOW) & (kj > qi)) | ((kj >= new0) & (kj - new0 <= qi)))[None]
    q = per_seq(q_ref[...])
    no_q = jnp.zeros((sb, dec_seq, HEAD_DIM), F32)
    head_rows = []
    for h in range(N_Q_HEADS):
        qh = q[:, :, h * HEAD_DIM:(h + 1) * HEAD_DIM]
        head_rows.append(jnp.concatenate([qh if g == h // Q_PER_KV else no_q for g in range(N_KV_HEADS)], axis=2))
    q_blocks = jnp.concatenate(head_rows, axis=1)
    s = _bdot(q_blocks.astype(BF16), keys_t)
    sink = _per_head([sink_ref[layer, h] for h in range(N_Q_HEADS)], dec_seq, axis=0)[None]
    p = _softmax(jnp.where(band, s, -jnp.inf), sink)
    o = _bdot_nt(p.astype(BF16), values_t)
    outs = []
    for h in range(N_Q_HEADS):
        g = h // Q_PER_KV
        outs.append(o[:, h * dec_seq:(h + 1) * dec_seq, g * HEAD_DIM:(g + 1) * HEAD_DIM])
    y_ref[:, pool_w:pool_w + swa_w] = flat(jnp.concatenate(outs, axis=2)).astype(BF16)

    xq = per_seq(xq_ref[...])
    x_w = xq.shape[2]
    n_xrows = N_X_HEADS * dec_seq
    row_head = lax.broadcasted_iota(jnp.int32, (n_xrows, x_w), 0) // dec_seq
    col_head = lax.broadcasted_iota(jnp.int32, (n_xrows, x_w), 1) // HEAD_DIM
    own = (row_head == col_head)[None]
    xq_blocks = jnp.where(own, jnp.concatenate([xq] * N_X_HEADS, axis=1), 0.0)
    p = _softmax(_bdot(xq_blocks.astype(BF16), cmkt_ref[...].astype(BF16)))
    o = jnp.where(own, _bdot_nt(p.astype(BF16), cmvt_ref[...].astype(BF16)), 0.0)
    y_x = o[:, 0:dec_seq, :]
    for h in range(1, N_X_HEADS):
        y_x = y_x + o[:, h * dec_seq:(h + 1) * dec_seq, :]
    y_ref[:, pool_w + swa_w:] = flat(y_x).astype(BF16)


def _mix_sample(sink, xp, q, k, v, xq, cache_kt, cache_vt, state16, cache_mkt, cache_mvt, pool_bd, pool_scale,
                layer, n_prompt, dec_seq, new_windows):
    depth, dec_batch = cache_kt.shape[:2]
    sb = SAMPLE_SEQS
    rows = sb * dec_seq
    off = n_prompt // rows
    pool_w, swa_w, kv_w, x_w = xp.shape[1], q.shape[1], k.shape[1], xq.shape[1]
    n_mem = cache_mkt.shape[3]
    d_out = pool_w + swa_w + x_w

    tok = lambda w: pl.BlockSpec((rows, w), lambda i: (off + i, 0))
    per_seq = lambda a, w: pl.BlockSpec((None, sb, a, w), lambda i: (layer, i, 0, 0))
    out_seq = lambda a, w: pl.BlockSpec((sb, a, w), lambda i: (i, 0, 0))
    e_shape = (sb, 2 * POOL_HALO + dec_seq, pool_w)
    n_fixed_inputs = 13
    return pl.pallas_call(
        functools.partial(_mix_sample_kernel, layer=layer, dec_seq=dec_seq),
        grid=(dec_batch // sb,),
        in_specs=[
            pl.BlockSpec(memory_space=pltpu.SMEM),
            tok(pool_w), tok(swa_w), tok(kv_w), tok(kv_w), tok(x_w),
            per_seq(kv_w, WINDOW), per_seq(kv_w, WINDOW), per_seq(POOL_HALO, pool_w),
            per_seq(x_w, n_mem), per_seq(x_w, n_mem),
            _resident((None, pool_w, pool_w), lambda i: (layer, 0, 0)),
            _resident((None, 1, pool_w), lambda i: (layer, 0, 0)),
        ] + [pl.BlockSpec(memory_space=pl.ANY)] * len(new_windows),
        out_specs=(
            pl.BlockSpec((rows, d_out), lambda i: (i, 0)),
            per_seq(kv_w, WINDOW), per_seq(kv_w, WINDOW), out_seq(POOL_HALO, pool_w),
        ),
        out_shape=(
            jax.ShapeDtypeStruct((dec_batch * dec_seq, d_out), BF16),
            jax.ShapeDtypeStruct((depth, dec_batch, kv_w, WINDOW), F32),
            jax.ShapeDtypeStruct((depth, dec_batch, kv_w, WINDOW), F32),
            jax.ShapeDtypeStruct((dec_batch, POOL_HALO, pool_w), F32),
        ),
        input_output_aliases={n_fixed_inputs + j: 1 + j for j in range(len(new_windows))},
        scratch_shapes=[
            pltpu.VMEM(e_shape, F32), pltpu.VMEM(e_shape, F32), pltpu.VMEM(e_shape, F32), pltpu.VMEM(e_shape, F32),
        ],
        compiler_params=_cparams("parallel"),
        name="mix_sample",
    )(sink, xp, q, k, v, xq, cache_kt, cache_vt, state16, cache_mkt, cache_mvt, pool_bd, pool_scale, *new_windows)


def _merge_ln_kernel(x_ref, yp_ref, ys_ref, win_ref, wbp_ref, wbs_ref, wbx_ref, wo_ref, g_ref, b_ref, *refs,
                     alpha, n_first_tiles, gate_col0):
    n_cast = (len(refs) - 1) // 2
    o_ref = refs[n_cast]
    _cast_blocks(refs[:n_cast], refs[n_cast + 1:])
    d = x_ref.shape[1]
    in_first = pl.program_id(0) < n_first_tiles

    def finish(rows, z):
        o_ref[rows, :] = _layer_norm(z, g_ref[...], b_ref[...])

    pending = None
    for r0 in range(0, x_ref.shape[0], SUB_TILE):
        rows = slice(r0, r0 + SUB_TILE)
        x = x_ref[rows, :]
        xb = x.astype(BF16)
        y = jnp.where(in_first, yp_ref[rows, :], ys_ref[rows, :])
        c0 = 0
        merged = None
        for b, w_ref in enumerate((wbp_ref, wbs_ref, wbx_ref)):
            w = w_ref.shape[0]
            gate = jax.nn.sigmoid(_dot(xb, win_ref[:, gate_col0 + b * d:gate_col0 + (b + 1) * d]))
            term = gate * _dot(y[:, c0:c0 + w], w_ref[...])
            merged = term if merged is None else merged + term
            c0 += w
            if b == 0 and pending is not None:
                finish(*pending)
                pending = None
        pending = (rows, alpha * x + _dot(merged.astype(BF16), wo_ref[...]))
    finish(*pending)


def _merge_ln(x, y_p, y_s, w_in, w_br_pool, w_br_swa, w_br_cross, w_o, ln_g, ln_b, layer, alpha, gate_col0,
              cast_weights, cast_layers):
    n, d = x.shape
    tm = WIDE_TOKEN_TILE
    n_first_tiles = y_p.shape[0] // tm
    first, second = _split_rows(n_first_tiles)
    lsel = lambda i: (layer, 0, 0)
    res = lambda w: _resident((None,) + w.shape[1:], lsel)
    cast_in, cast_out, cast_shape = _cast_specs(cast_weights, cast_layers,
                                                lambda i: jnp.minimum(i, N_CAST_BLOCKS - 1))
    assert n // tm >= N_CAST_BLOCKS
    out = pl.pallas_call(
        functools.partial(_merge_ln_kernel, alpha=alpha, n_first_tiles=n_first_tiles, gate_col0=gate_col0),
        grid=(n // tm,),
        in_specs=[
            pl.BlockSpec((tm, d), lambda i: (i, 0)),
            pl.BlockSpec((tm, y_p.shape[1]), first),
            pl.BlockSpec((tm, y_s.shape[1]), second),
            _resident(w_in.shape, lambda i: (0, 0)),
            res(w_br_pool), res(w_br_swa), res(w_br_cross), res(w_o), res(ln_g), res(ln_b),
        ] + cast_in,
        out_specs=[pl.BlockSpec((tm, d), lambda i: (i, 0))] + cast_out,
        out_shape=[jax.ShapeDtypeStruct((n, d), F32)] + cast_shape,
        compiler_params=_cparams("arbitrary"),
        name="merge_ln",
    )(x, y_p, y_s, w_in, w_br_pool, w_br_swa, w_br_cross, w_o, ln_g, ln_b, *cast_weights)
    return out[0], tuple(out[1:])


def kernel(x_prompt, x_sample, cache_swa_k, cache_swa_v, state_pool, cache_mem_k, cache_mem_v, mem_prompt, w_mem_k, w_mem_v, ffn1_w_gate, ffn1_w_up, ffn1_w_down, ln1_g, ln1_b, w_in, pool_mix, pool_scale, attn_sink, w_br_pool, w_br_swa, w_br_cross, w_o, ln2_g, ln2_b, ffn2_w_gate, ffn2_w_up, ffn2_w_down, ln3_g, ln3_b):
    batch, seq, d = x_prompt.shape
    dec_batch, dec_seq, _ = x_sample.shape
    depth = w_in.shape[0]
    n_groups, pool_group = pool_mix.shape[1], pool_mix.shape[2]
    pool_w = n_groups * pool_group
    swa_w = w_br_swa.shape[1]
    kv_w = cache_swa_k.shape[3] * cache_swa_k.shape[4]
    x_w = w_br_cross.shape[1]
    n_mem = mem_prompt.shape[1]
    widths = (pool_w, swa_w, kv_w, x_w)
    proj_w = sum(widths) + kv_w
    assert pool_group == POOL_GROUP and swa_w == N_Q_HEADS * HEAD_DIM and kv_w == N_KV_HEADS * HEAD_DIM
    assert x_w == N_X_HEADS * HEAD_DIM and cache_swa_k.shape[2] == WINDOW
    assert state_pool.shape[2] == POOL_HALO - 1 and w_in.shape[2] == proj_w + 3 * d
    alpha = (2 * depth) ** 0.25
    n_prompt, n_sample = batch * seq, dec_batch * dec_seq

    bf = lambda w: w.astype(BF16)
    vec = lambda p: p.reshape(depth, 1, p.shape[-1])
    pool_bd = bf(jnp.einsum("lgcd,gh->lgchd", pool_mix, jnp.eye(n_groups, dtype=F32))
                 .reshape(depth, pool_w, pool_w))
    state16 = jnp.pad(state_pool, ((0, 0), (0, 0), (1, 0), (0, 0)))
    to_cols = lambda c: c.transpose(0, 1, 3, 4, 2).reshape(depth, dec_batch, c.shape[3] * HEAD_DIM, c.shape[2])
    from_cols = lambda c, n_h: c.reshape(depth, dec_batch, n_h, HEAD_DIM, c.shape[-1]).transpose(0, 1, 4, 2, 3)
    cache_kt, cache_vt = to_cols(cache_swa_k), to_cols(cache_swa_v)
    cache_mkt, cache_mvt = to_cols(cache_mem_k), to_cols(cache_mem_v)

    tables = _rope_tables(seq, n_sample, dec_seq)
    mk, mv, mkh, mvt = _mem_kv(bf(mem_prompt.reshape(batch * n_mem, d)), bf(w_mem_k), bf(w_mem_v), batch)

    ffn1_f32 = (ffn1_w_gate, ffn1_w_up, ffn1_w_down)
    ffn2_f32 = (ffn2_w_gate, ffn2_w_up, ffn2_w_down)
    ffn1_w = tuple(bf(w[0]) for w in ffn1_f32)
    w_in_l = bf(w_in[0])
    ffn1_ln = (vec(ln1_g), vec(ln1_b))
    ffn2_ln = (vec(ln3_g), vec(ln3_b))
    merge = (bf(w_br_pool), bf(w_br_swa), bf(w_br_cross), bf(w_o), vec(ln2_g), vec(ln2_b))
    pscale = vec(pool_scale)

    xs = (x_prompt.reshape(n_prompt, d), x_sample.reshape(n_sample, d))
    pk, pv, pp, sp = [], [], [], []
    new_windows = (jnp.zeros(cache_kt.shape, F32), jnp.zeros(cache_vt.shape, F32))
    for l in range(depth):
        more = l + 1 < depth
        x, xp, q, k, v, xq = _ffn(xs, *ffn1_w, *ffn1_ln, l, alpha, n_prompt, proj=(w_in_l, tables, seq, widths))
        y_p, nk, nv, npool_p, ffn1_w = _mix_prompt(attn_sink, xp, q, k, v, xq, mkh, mvt, pool_bd, pscale, l,
                                                   batch, seq, cast_weights=ffn1_f32 if more else (),
                                                   cast_layer=l + 1)
        y_s, *new_windows, npool_s = _mix_sample(attn_sink, xp, q, k, v, xq, cache_kt, cache_vt, state16,
                                                 cache_mkt, cache_mvt, pool_bd, pscale, l, n_prompt, dec_seq,
                                                 new_windows)
        x, casts = _merge_ln(x, y_p, y_s, w_in_l, *merge, l, alpha, proj_w,
                             cast_weights=ffn2_f32 + ((w_in,) if more else ()), cast_layers=(l, l, l, l + 1))
        w_in_l = casts[3] if more else None
        xs = _ffn((x,), *casts[:3], *ffn2_ln, l, alpha, n_prompt, split_out=not more)
        pk.append(nk)
        pv.append(nv)
        pp.append(npool_p[:, 1:])
        sp.append(npool_s[:, 1:])

    heads = lambda a, n_h: a.reshape(a.shape[:-1] + (n_h, HEAD_DIM))
    return (xs[0].reshape(batch, seq, d), xs[1].reshape(dec_batch, dec_seq, d),
            heads(jnp.stack(pk), N_KV_HEADS), heads(jnp.stack(pv), N_KV_HEADS), jnp.stack(pp),
            heads(mk.reshape(depth, batch, n_mem, x_w), N_X_HEADS),
            heads(mv.reshape(depth, batch, n_mem, x_w), N_X_HEADS),
            from_cols(new_windows[0], N_KV_HEADS), from_cols(new_windows[1], N_KV_HEADS), jnp.stack(sp))
```

```python
import functools

import jax
import jax.numpy as jnp
from jax import lax
from jax.experimental import pallas as pl
from jax.experimental.pallas import tpu as pltpu

F32 = jnp.float32
BF16 = jnp.bfloat16

HEAD_DIM = 64
N_Q_HEADS = 8
N_KV_HEADS = 2
Q_PER_KV = N_Q_HEADS // N_KV_HEADS
N_X_HEADS = 4
WINDOW = 128
ROPE_THETA = 500000.0
ROT_DIM = 16
ROT_HALF = ROT_DIM // 2
POOL_GROUP = 64
POOL_HALO = 16
PAST_LEN = 8192
LN_EPS = 1e-5
SM_SCALE = HEAD_DIM ** -0.5

VMEM_LIMIT_BYTES_V7X = 56 * 1024 * 1024
LANES = 128

TOKEN_TILE = 512
WIDE_TOKEN_TILE = 1024
SUB_TILE = 256
FF_CHUNK = 256
SAMPLE_SEQS = 16


def _cparams(*semantics):
    return pltpu.CompilerParams(dimension_semantics=semantics, vmem_limit_bytes=VMEM_LIMIT_BYTES_V7X)


def _resident(shape, index_map):
    return pl.BlockSpec(shape, index_map, pipeline_mode=pl.Buffered(1))


def _dot(a, b):
    return jnp.dot(a, b, preferred_element_type=F32)


def _dot_nt(a, b):
    return lax.dot_general(a, b, (((1,), (1,)), ((), ())), preferred_element_type=F32)


def _bdot_nt(a, b):
    return lax.dot_general(a, b, (((2,), (2,)), ((0,), (0,))), preferred_element_type=F32)


def _bdot(a, b):
    return lax.dot_general(a, b, (((2,), (1,)), ((0,), (0,))), preferred_element_type=F32)


def _layer_norm(z, g, b):
    mu = jnp.mean(z, axis=-1, keepdims=True)
    zc = z - mu
    var = jnp.mean(zc * zc, axis=-1, keepdims=True)
    return zc * lax.rsqrt(var + LN_EPS) * g + b


N_CAST_BLOCKS = 16


def _cast_specs(weights, layers, block_of_step):
    in_specs, out_specs, out_shape = [], [], []
    for w, layer in zip(weights, layers):
        _, r, c = w.shape
        rb = r // N_CAST_BLOCKS
        assert rb * N_CAST_BLOCKS == r and rb % 16 == 0
        in_specs.append(pl.BlockSpec((None, rb, c), lambda *ids, layer=layer: (layer, block_of_step(*ids), 0)))
        out_specs.append(pl.BlockSpec((rb, c), lambda *ids: (block_of_step(*ids), 0)))
        out_shape.append(jax.ShapeDtypeStruct((r, c), BF16))
    return in_specs, out_specs, out_shape


def _cast_blocks(src_refs, dst_refs):
    for src, dst in zip(src_refs, dst_refs):
        dst[...] = src[...].astype(BF16)


def _split_rows(n_first_tiles):
    first = lambda i: (jnp.minimum(i, n_first_tiles - 1), 0)
    second = lambda i: (jnp.maximum(i - n_first_tiles, 0), 0)
    return first, second


def _rope_table_kernel(cos_ref, sp_ref, sm_ref, *, seq, dec_seq):
    n = cos_ref.shape[0]
    r = lax.broadcasted_iota(jnp.int32, (n, LANES), 0)
    lane = lax.broadcasted_iota(jnp.int32, (n, LANES), 1)
    pos = jnp.where(r < seq, r, PAST_LEN + (r - seq) % dec_seq)
    d = lane % HEAD_DIM
    fidx = (d % ROT_HALF).astype(F32)
    inv_freq = jnp.power(ROPE_THETA, -fidx * (2.0 / ROT_DIM))
    ang = pos.astype(F32) * inv_freq
    cos = jnp.cos(ang)
    sin = jnp.sin(ang)
    cos_ref[...] = jnp.where(d < ROT_DIM, cos, 1.0)
    sp_ref[...] = jnp.where((d >= ROT_HALF) & (d < ROT_DIM), sin, 0.0)
    sm_ref[...] = jnp.where(d < ROT_HALF, -sin, 0.0)


def _rope_tables(seq, n_sample_rows, dec_seq):
    n = seq + n_sample_rows
    out = jax.ShapeDtypeStruct((n, LANES), F32)
    return pl.pallas_call(
        functools.partial(_rope_table_kernel, seq=seq, dec_seq=dec_seq),
        out_shape=(out, out, out),
        name="rope_tables",
    )()


def _mem_kv_kernel(m_ref, wk_ref, wv_ref, k_ref, v_ref, kh_ref, vt_ref):
    m = m_ref[...]
    k = _dot(m, wk_ref[...])
    v = _dot(m, wv_ref[...])
    k_ref[...] = k
    v_ref[...] = v
    batch, n_heads, n_mem, _ = kh_ref.shape
    for b in range(batch):
        rows = slice(b * n_mem, (b + 1) * n_mem)
        for h in range(n_heads):
            kh_ref[b, h] = k[rows, h * HEAD_DIM:(h + 1) * HEAD_DIM].astype(BF16)
        vt_ref[b] = v[rows, :].T.astype(BF16)


def _mem_kv(mem, w_k, w_v, batch):
    depth, d, xw = w_k.shape
    rows = mem.shape[0]
    n_mem = rows // batch
    n_heads = xw // HEAD_DIM
    out = jax.ShapeDtypeStruct((depth, rows, xw), F32)
    w_spec = pl.BlockSpec((None, d, xw), lambda l: (l, 0, 0))
    o_spec = pl.BlockSpec((None, rows, xw), lambda l: (l, 0, 0))
    return pl.pallas_call(
        _mem_kv_kernel,
        grid=(depth,),
        in_specs=[pl.BlockSpec((rows, d), lambda l: (0, 0)), w_spec, w_spec],
        out_specs=(
            o_spec, o_spec,
            pl.BlockSpec((None, batch, n_heads, n_mem, HEAD_DIM), lambda l: (l, 0, 0, 0, 0)),
            pl.BlockSpec((None, batch, xw, n_mem), lambda l: (l, 0, 0, 0)),
        ),
        out_shape=(
            out, out,
            jax.ShapeDtypeStruct((depth, batch, n_heads, n_mem, HEAD_DIM), BF16),
            jax.ShapeDtypeStruct((depth, batch, xw, n_mem), BF16),
        ),
        compiler_params=_cparams("parallel"),
        name="mem_kv",
    )(mem, w_k, w_v)


def _ffn_kernel(*refs, alpha, n_first_tiles, split_in, split_out, proj_widths):
    refs = list(refs)
    x_refs = [refs.pop(0) for _ in range(2 if split_in else 1)]
    wg_ref, wu_ref, wd_ref, g_ref, b_ref = (refs.pop(0) for _ in range(5))
    if proj_widths:
        wp_ref, cos_ref, sp_ref, sm_ref = (refs.pop(0) for _ in range(4))
    o_refs = [refs.pop(0) for _ in range(2 if split_out else 1)]
    if proj_widths:
        xp_ref, q_ref, k_ref, v_ref, xq_ref = (refs.pop(0) for _ in range(5))
    (a_ref,) = refs
    i = pl.program_id(0)
    d_ff = wg_ref.shape[1]

    def finish(rows, z):
        out = _layer_norm(z, g_ref[...], b_ref[...])

        if split_out:
            @pl.when(i < n_first_tiles)
            def _():
                o_refs[0][rows, :] = out
            o_refs[1][rows, :] = out
        else:
            o_refs[0][rows, :] = out

        if proj_widths:
            pool_w, swa_w, kv_w, x_w = proj_widths
            ob = out.astype(BF16)
            cos, sp, sm = cos_ref[rows, :], sp_ref[rows, :], sm_ref[rows, :]

            def rope(t):
                return t * cos + pltpu.roll(t, ROT_HALF, 1) * sp + pltpu.roll(t, LANES - ROT_HALF, 1) * sm

            h = _dot(ob, wp_ref[...])
            c0 = 0
            xp_ref[rows, :] = h[:, c0:c0 + pool_w]
            c0 += pool_w
            for s in range(swa_w // LANES):
                q_ref[rows, s * LANES:(s + 1) * LANES] = rope(h[:, c0 + s * LANES:c0 + (s + 1) * LANES]) * SM_SCALE
            c0 += swa_w
            k_ref[rows, :] = rope(h[:, c0:c0 + kv_w])
            v_ref[rows, :] = h[:, c0 + kv_w:c0 + 2 * kv_w]
            c0 += 2 * kv_w
            xq_ref[rows, :] = h[:, c0:c0 + x_w] * SM_SCALE

    pending = None
    for r0 in range(0, a_ref.shape[0], SUB_TILE):
        rows = slice(r0, r0 + SUB_TILE)
        x = x_refs[0][rows, :]
        if split_in:
            x = jnp.where(i < n_first_tiles, x, x_refs[1][rows, :])
        xb = x.astype(BF16)
        for c in range(0, d_ff, FF_CHUNK):
            g = _dot(xb, wg_ref[:, c:c + FF_CHUNK])
            u = _dot(xb, wu_ref[:, c:c + FF_CHUNK])
            a_ref[rows, c:c + FF_CHUNK] = (g * jax.nn.sigmoid(g) * u).astype(BF16)
            if c == FF_CHUNK and pending is not None:
                finish(*pending)
                pending = None
        y = _dot(a_ref[rows, :], wd_ref[...])
        pending = (rows, alpha * x + 0.5 * y)
    finish(*pending)


def _ffn(xs,w_gate, w_up, w_down, ln_g, ln_b, layer, alpha, n_prompt, split_out=False, proj=None, tiles=None):
    split_in = len(xs) == 2
    d = xs[0].shape[1]
    d_ff = w_gate.shape[1]
    tm = TOKEN_TILE if proj is not None else WIDE_TOKEN_TILE
    first_tile, n_tiles = tiles if tiles is not None else (0, sum(x.shape[0] for x in xs) // tm)
    n = n_tiles * tm
    n_first_tiles = n_prompt // tm
    first, second = _split_rows(n_first_tiles)
    whole = lambda i: (i, 0)
    lsel = lambda i: (layer, 0, 0)

    in_specs = ([pl.BlockSpec((tm, d), first), pl.BlockSpec((tm, d), second)] if split_in
                else [pl.BlockSpec((tm, d), lambda i: (first_tile + i, 0))])
    in_specs += [
        _resident((d, d_ff), lambda i: (0, 0)), _resident((d, d_ff), lambda i: (0, 0)),
        _resident((d_ff, d), lambda i: (0, 0)),
        _resident((None, 1, d), lsel), _resident((None, 1, d), lsel),
    ]
    args = list(xs) + [w_gate, w_up, w_down, ln_g, ln_b]
    if split_out:
        out_specs = [pl.BlockSpec((tm, d), first), pl.BlockSpec((tm, d), second)]
        out_shape = [jax.ShapeDtypeStruct((n_prompt, d), F32), jax.ShapeDtypeStruct((n - n_prompt, d), F32)]
    else:
        out_specs = [pl.BlockSpec((tm, d), whole)]
        out_shape = [jax.ShapeDtypeStruct((n, d), F32)]
    widths = None
    if proj is not None:
        w_in, tables, seq, widths = proj
        pool_w, swa_w, kv_w, x_w = widths
        assert kv_w == LANES
        seq_tiles = seq // tm

        def table_idx(i):
            return (jnp.where(i < n_first_tiles, i % seq_tiles, seq_tiles + i - n_first_tiles), 0)

        t_spec = pl.BlockSpec((tm, LANES), table_idx)
        in_specs += [_resident((d, sum(widths) + kv_w), lambda i: (0, 0)), t_spec, t_spec, t_spec]
        args += [w_in, *tables]
        for w in (pool_w, swa_w, kv_w, kv_w, x_w):
            out_specs.append(pl.BlockSpec((tm, w), whole))
            out_shape.append(jax.ShapeDtypeStruct((n, w), F32))
    return pl.pallas_call(
        functools.partial(_ffn_kernel, alpha=alpha, n_first_tiles=n_first_tiles, split_in=split_in,
                          split_out=split_out, proj_widths=widths),
        grid=(n // tm,),
        in_specs=in_specs,
        out_specs=out_specs,
        out_shape=out_shape,
        scratch_shapes=[pltpu.VMEM((tm, d_ff), BF16)],
        compiler_params=_cparams("arbitrary"),
        name="ffn_proj" if proj is not None else "ffn",
    )(*args)


def _pool_window_sums(e_ref, w2_ref, w4_ref, w8_ref, n):
    h = 2 * POOL_HALO
    lead = (slice(None),) * (len(e_ref.shape) - 2)
    rows = lambda a, b: lead + (slice(a, b), slice(None))
    sl = lambda ref, a, b: ref[rows(a, b)]
    w2_ref[rows(8, n)] = sl(e_ref, 8, n) + sl(e_ref, 7, n - 1)
    w4_ref[rows(16, n)] = sl(w2_ref, 16, n) + sl(w2_ref, 14, n - 2)
    w8_ref[rows(24, n)] = sl(w4_ref, 24, n) + sl(w4_ref, 20, n - 4)
    w16 = sl(w8_ref, h, n) + sl(w8_ref, h - 8, n - 8)
    return sl(w2_ref, h, n), sl(w4_ref, h, n), sl(w8_ref, h, n), w16


def _pool_delta(sums, xp, pos):
    w2, w4, w8, w16 = sums
    grp = lax.broadcasted_iota(jnp.int32, xp.shape, xp.ndim - 1) // POOL_GROUP
    s = jnp.where(grp == 0, w2, jnp.where(grp == 1, w4, jnp.where(grp == 2, w8, w16)))
    width = jnp.left_shift(2, grp)
    cnt = jnp.minimum(width, pos + 1).astype(F32)
    return s / cnt - xp


def _per_head(values, n_per_head, axis):
    n = len(values) * n_per_head
    shape = (n, 1) if axis == 0 else (1, n)
    head = lax.broadcasted_iota(jnp.int32, shape, axis) // n_per_head
    vec = values[-1]
    for i in reversed(range(len(values) - 1)):
        vec = jnp.where(head <= i, values[i], vec)
    return vec


def _softmax(s, sink=None, axis=-1):
    m = jnp.max(s, axis=axis, keepdims=True)
    if sink is not None:
        m = jnp.maximum(m, sink)
    p = jnp.exp(s - m)
    den = jnp.sum(p, axis=axis, keepdims=True)
    if sink is not None:
        den = den + jnp.exp(sink - m)
    return p * (1.0 / den)


def _mix_prompt_kernel(sink_ref, xp_ref, xph_ref, q_ref, k_ref, kh_ref, v_ref, vh_ref, xq_ref,
                       mkh_ref, mvt_ref, pbd_ref, pscale_ref, *refs, layer):
    n_cast = (len(refs) - 10) // 2
    y_ref, nk_ref, nv_ref, npool_ref = refs[n_cast:n_cast + 4]
    e_ref, w2_ref, w4_ref, w8_ref, kb_ref, vt_ref = refs[2 * n_cast + 4:]
    _cast_blocks(refs[:n_cast], refs[n_cast + 4:2 * n_cast + 4])
    t = pl.program_id(1)
    tq = xp_ref.shape[0]
    pool_w = xp_ref.shape[1]
    swa_w = q_ref.shape[1]

    nk_ref[...] = k_ref[tq - WINDOW:, :]
    nv_ref[...] = v_ref[tq - WINDOW:, :]
    npool_ref[...] = xp_ref[tq - POOL_HALO:, :]

    xp = xp_ref[...]
    e_ref[0:POOL_HALO, :] = jnp.zeros((POOL_HALO, pool_w), F32)
    e_ref[POOL_HALO:2 * POOL_HALO, :] = jnp.where(t == 0, 0.0, xph_ref[...])
    e_ref[2 * POOL_HALO:, :] = xp
    sums = _pool_window_sums(e_ref, w2_ref, w4_ref, w8_ref, tq + 2 * POOL_HALO)
    pos = t * tq + lax.broadcasted_iota(jnp.int32, (tq, pool_w), 0)
    delta = _pool_delta(sums, xp, pos)
    y_pool = _dot(delta.astype(BF16), pbd_ref[...]) * pscale_ref[...]
    y_ref[:, 0:pool_w] = y_pool.astype(BF16)

    def store_pair(top, bottom, rows, col0):
        y_ref[rows, col0:col0 + 2 * HEAD_DIM] = jnp.concatenate([top, bottom], axis=0).T.astype(BF16)

    for g in range(N_KV_HEADS):
        cs = slice(g * HEAD_DIM, (g + 1) * HEAD_DIM)
        kb_ref[g, 0:WINDOW, :] = kh_ref[:, cs].astype(BF16)
        kb_ref[g, WINDOW:, :] = k_ref[:, cs].astype(BF16)
    vt_ref[:, 0:WINDOW] = vh_ref[...].T.astype(BF16)
    vt_ref[:, WINDOW:] = v_ref[...].T.astype(BF16)
    n_cols = Q_PER_KV * WINDOW
    qi = lax.broadcasted_iota(jnp.int32, (WINDOW, n_cols), 1) % WINDOW
    from_prev = lax.broadcasted_iota(jnp.int32, (WINDOW, n_cols), 0) > qi
    for j in range(tq // WINDOW):
        rows = slice(j * WINDOW, (j + 1) * WINDOW)
        keys = slice(j * WINDOW, (j + 2) * WINDOW)
        for g in range(N_KV_HEADS):
            heads = range(g * Q_PER_KV, (g + 1) * Q_PER_KV)
            qg = jnp.concatenate([q_ref[rows, h * HEAD_DIM:(h + 1) * HEAD_DIM] for h in heads], axis=0)
            s = _dot_nt(kb_ref[g, keys, :], qg.astype(BF16))
            s_prev = s[:WINDOW, :]
            if j == 0:
                s_prev = jnp.where(t > 0, s_prev, -jnp.inf)
            sink = _per_head([sink_ref[layer, h] for h in heads], WINDOW, axis=1)
            p = _softmax(jnp.where(from_prev, s_prev, s[WINDOW:, :]), sink, axis=0)
            p2 = jnp.concatenate([jnp.where(from_prev, p, 0.0), jnp.where(from_prev, 0.0, p)], axis=0)
            o = _dot(vt_ref[g * HEAD_DIM:(g + 1) * HEAD_DIM, keys], p2.astype(BF16))
            for i in range(0, Q_PER_KV, 2):
                store_pair(o[:, i * WINDOW:(i + 1) * WINDOW], o[:, (i + 1) * WINDOW:(i + 2) * WINDOW],
                           rows, pool_w + (heads[0] + i) * HEAD_DIM)

    outs = []
    for h in range(N_X_HEADS):
        cs = slice(h * HEAD_DIM, (h + 1) * HEAD_DIM)
        p = _softmax(_dot_nt(mkh_ref[h], xq_ref[:, cs].astype(BF16)), axis=0)
        outs.append(_dot(mvt_ref[cs, :], p.astype(BF16)))
    for h in range(0, N_X_HEADS, 2):
        store_pair(outs[h], outs[h + 1], slice(None), pool_w + swa_w + h * HEAD_DIM)


def _mix_prompt(sink, xp, q, k, v, xq, mkh, mvt, pool_bd, pool_scale, layer, batch, seq,
                cast_weights=(), cast_layer=0):
    tq = WIDE_TOKEN_TILE
    n_t = seq // tq
    pool_w, swa_w, kv_w, x_w = xp.shape[1], q.shape[1], k.shape[1], xq.shape[1]
    n_mem = mvt.shape[3]
    d_out = pool_w + swa_w + x_w

    tile = lambda w: pl.BlockSpec((tq, w), lambda b, t: (b * n_t + t, 0))

    def halo(rows, w):
        per_tile = tq // rows
        return pl.BlockSpec((rows, w), lambda b, t: (jnp.maximum((b * n_t + t) * per_tile - 1, 0), 0))

    mem_k = pl.BlockSpec((None, None, N_X_HEADS, n_mem, HEAD_DIM), lambda b, t: (layer, b, 0, 0, 0))
    mem_v = pl.BlockSpec((None, None, x_w, n_mem), lambda b, t: (layer, b, 0, 0))
    steps_per_block = (batch * n_t) // N_CAST_BLOCKS
    assert steps_per_block * N_CAST_BLOCKS == batch * n_t
    cast_in, cast_out, cast_shape = _cast_specs(cast_weights, [cast_layer] * len(cast_weights),
                                                lambda b, t: (b * n_t + t) // steps_per_block)
    out = pl.pallas_call(
        functools.partial(_mix_prompt_kernel, layer=layer),
        grid=(batch, n_t),
        in_specs=[
            pl.BlockSpec(memory_space=pltpu.SMEM),
            tile(pool_w), halo(POOL_HALO, pool_w),
            tile(swa_w),
            tile(kv_w), halo(WINDOW, kv_w),
            tile(kv_w), halo(WINDOW, kv_w),
            tile(x_w),
            mem_k, mem_v,
            _resident((None, pool_w, pool_w), lambda b, t: (layer, 0, 0)),
            _resident((None, 1, pool_w), lambda b, t: (layer, 0, 0)),
        ] + cast_in,
        out_specs=[
            pl.BlockSpec((tq, d_out), lambda b, t: (b * n_t + t, 0)),
            pl.BlockSpec((None, WINDOW, kv_w), lambda b, t: (b, 0, 0)),
            pl.BlockSpec((None, WINDOW, kv_w), lambda b, t: (b, 0, 0)),
            pl.BlockSpec((None, POOL_HALO, pool_w), lambda b, t: (b, 0, 0)),
        ] + cast_out,
        out_shape=[
            jax.ShapeDtypeStruct((batch * seq, d_out), BF16),
            jax.ShapeDtypeStruct((batch, WINDOW, kv_w), F32),
            jax.ShapeDtypeStruct((batch, WINDOW, kv_w), F32),
            jax.ShapeDtypeStruct((batch, POOL_HALO, pool_w), F32),
        ] + cast_shape,
        scratch_shapes=[
            pltpu.VMEM((tq + 2 * POOL_HALO, pool_w), F32),
            pltpu.VMEM((tq + 2 * POOL_HALO, pool_w), F32),
            pltpu.VMEM((tq + 2 * POOL_HALO, pool_w), F32),
            pltpu.VMEM((tq + 2 * POOL_HALO, pool_w), F32),
            pltpu.VMEM((N_KV_HEADS, tq + WINDOW, HEAD_DIM), BF16),
            pltpu.VMEM((kv_w, tq + WINDOW), BF16),
        ],
        compiler_params=_cparams("arbitrary", "arbitrary"),
        name="mix_prompt",
    )(sink, xp, xp, q, k, k, v, v, xq, mkh, mvt, pool_bd, pool_scale, *cast_weights)
    return out[0], out[1], out[2], out[3], tuple(out[4:])


def _mix_sample_kernel(sink_ref, xp_ref, q_ref, k_ref, v_ref, xq_ref, ckt_ref, cvt_ref, st_ref,
                       cmkt_ref, cmvt_ref, pbd_ref, pscale_ref, *refs, layer, dec_seq):
    y_ref, nkt_ref, nvt_ref, npool_ref, e_ref, w2_ref, w4_ref, w8_ref = refs[-8:]
    sb = ckt_ref.shape[0]
    pool_w = xp_ref.shape[1]
    swa_w = q_ref.shape[1]
    kv_w = k_ref.shape[1]
    h2 = 2 * POOL_HALO
    per_seq = lambda a: a.reshape(sb, dec_seq, a.shape[-1])
    flat = lambda a: a.reshape(a.shape[0] * a.shape[1], a.shape[2])

    xp = per_seq(xp_ref[...])
    e_ref[:, 0:POOL_HALO, :] = jnp.zeros((sb, POOL_HALO, pool_w), F32)
    e_ref[:, POOL_HALO:h2, :] = st_ref[...]
    e_ref[:, h2:, :] = xp
    sums = _pool_window_sums(e_ref, w2_ref, w4_ref, w8_ref, h2 + dec_seq)
    pos = PAST_LEN + lax.broadcasted_iota(jnp.int32, xp.shape, 1)
    delta = flat(_pool_delta(sums, xp, pos))
    y_pool = _dot(delta.astype(BF16), pbd_ref[...]) * pscale_ref[...]
    y_ref[:, 0:pool_w] = y_pool.astype(BF16)
    npool_ref[...] = e_ref[:, h2 + dec_seq - POOL_HALO:, :]

    first_new = WINDOW - dec_seq
    lane = lax.broadcasted_iota(jnp.int32, (sb * kv_w, WINDOW), 1)
    keys_and_values = []
    for cache_ref, tok_ref, out_ref in ((ckt_ref, k_ref, nkt_ref), (cvt_ref, v_ref, nvt_ref)):
        old = cache_ref[...]
        tok_rows = jnp.concatenate([jnp.zeros((sb, first_new, kv_w), F32), per_seq(tok_ref[...])], axis=1)
        new = jnp.swapaxes(tok_rows, 1, 2)
        shifted = jnp.where(lane >= first_new, flat(new), pltpu.roll(flat(old), first_new, 1))
        out_ref[...] = shifted.reshape(sb, kv_w, WINDOW)
        keys_and_values.append(jnp.concatenate([old, new], axis=2).astype(BF16))
    keys_t, values_t = keys_and_values

    n_keys = 2 * WINDOW
    n_rows = N_Q_HEADS * dec_seq
    qi = lax.broadcasted_iota(jnp.int32, (n_rows, n_keys), 0) % dec_seq
    kj = lax.broadcasted_iota(jnp.int32, (n_rows, n_keys), 1)
    new0 = WINDOW + first_new
    band = (((kj < WINDOW) & (kj > qi)) | ((kj >= new0) & (kj - new0 <= qi)))[None]
    q = per_seq(q_ref[...])
    no_q = jnp.zeros((sb, dec_seq, HEAD_DIM), F32)
    head_rows = []
    for h in range(N_Q_HEADS):
        qh = q[:, :, h * HEAD_DIM:(h + 1) * HEAD_DIM]
        head_rows.append(jnp.concatenate([qh if g == h // Q_PER_KV else no_q for g in range(N_KV_HEADS)], axis=2))
    q_blocks = jnp.concatenate(head_rows, axis=1)
    s = _bdot(q_blocks.astype(BF16), keys_t)
    sink = _per_head([sink_ref[layer, h] for h in range(N_Q_HEADS)], dec_seq, axis=0)[None]
    p = _softmax(jnp.where(band, s, -jnp.inf), sink)
    o = _bdot_nt(p.astype(BF16), values_t)
    outs = []
    for h in range(N_Q_HEADS):
        g = h // Q_PER_KV
        outs.append(o[:, h * dec_seq:(h + 1) * dec_seq, g * HEAD_DIM:(g + 1) * HEAD_DIM])
    y_ref[:, pool_w:pool_w + swa_w] = flat(jnp.concatenate(outs, axis=2)).astype(BF16)

    xq = per_seq(xq_ref[...])
    x_w = xq.shape[2]
    n_xrows = N_X_HEADS * dec_seq
    row_head = lax.broadcasted_iota(jnp.int32, (n_xrows, x_w), 0) // dec_seq
    col_head = lax.broadcasted_iota(jnp.int32, (n_xrows, x_w), 1) // HEAD_DIM
    own = (row_head == col_head)[None]
    xq_blocks = jnp.where(own, jnp.concatenate([xq] * N_X_HEADS, axis=1), 0.0)
    p = _softmax(_bdot(xq_blocks.astype(BF16), cmkt_ref[...].astype(BF16)))
    o = jnp.where(own, _bdot_nt(p.astype(BF16), cmvt_ref[...].astype(BF16)), 0.0)
    y_x = o[:, 0:dec_seq, :]
    for h in range(1, N_X_HEADS):
        y_x = y_x + o[:, h * dec_seq:(h + 1) * dec_seq, :]
    y_ref[:, pool_w + swa_w:] = flat(y_x).astype(BF16)


def _mix_sample(sink, xp, q, k, v, xq, cache_kt, cache_vt, state16, cache_mkt, cache_mvt, pool_bd, pool_scale,
                layer, n_prompt, dec_seq, new_windows):
    depth, dec_batch = cache_kt.shape[:2]
    sb = SAMPLE_SEQS
    rows = sb * dec_seq
    off = n_prompt // rows
    pool_w, swa_w, kv_w, x_w = xp.shape[1], q.shape[1], k.shape[1], xq.shape[1]
    n_mem = cache_mkt.shape[3]
    d_out = pool_w + swa_w + x_w

    tok = lambda w: pl.BlockSpec((rows, w), lambda i: (off + i, 0))
    per_seq = lambda a, w: pl.BlockSpec((None, sb, a, w), lambda i: (layer, i, 0, 0))
    out_seq = lambda a, w: pl.BlockSpec((sb, a, w), lambda i: (i, 0, 0))
    e_shape = (sb, 2 * POOL_HALO + dec_seq, pool_w)
    n_fixed_inputs = 13
    return pl.pallas_call(
        functools.partial(_mix_sample_kernel, layer=layer, dec_seq=dec_seq),
        grid=(dec_batch // sb,),
        in_specs=[
            pl.BlockSpec(memory_space=pltpu.SMEM),
            tok(pool_w), tok(swa_w), tok(kv_w), tok(kv_w), tok(x_w),
            per_seq(kv_w, WINDOW), per_seq(kv_w, WINDOW), per_seq(POOL_HALO, pool_w),
            per_seq(x_w, n_mem), per_seq(x_w, n_mem),
            _resident((None, pool_w, pool_w), lambda i: (layer, 0, 0)),
            _resident((None, 1, pool_w), lambda i: (layer, 0, 0)),
        ] + [pl.BlockSpec(memory_space=pl.ANY)] * len(new_windows),
        out_specs=(
            pl.BlockSpec((rows, d_out), lambda i: (i, 0)),
            per_seq(kv_w, WINDOW), per_seq(kv_w, WINDOW), out_seq(POOL_HALO, pool_w),
        ),
        out_shape=(
            jax.ShapeDtypeStruct((dec_batch * dec_seq, d_out), BF16),
            jax.ShapeDtypeStruct((depth, dec_batch, kv_w, WINDOW), F32),
            jax.ShapeDtypeStruct((depth, dec_batch, kv_w, WINDOW), F32),
            jax.ShapeDtypeStruct((dec_batch, POOL_HALO, pool_w), F32),
        ),
        input_output_aliases={n_fixed_inputs + j: 1 + j for j in range(len(new_windows))},
        scratch_shapes=[
            pltpu.VMEM(e_shape, F32), pltpu.VMEM(e_shape, F32), pltpu.VMEM(e_shape, F32), pltpu.VMEM(e_shape, F32),
        ],
        compiler_params=_cparams("parallel"),
        name="mix_sample",
    )(sink, xp, q, k, v, xq, cache_kt, cache_vt, state16, cache_mkt, cache_mvt, pool_bd, pool_scale, *new_windows)


def _merge_ln_kernel(x_ref, yp_ref, ys_ref, win_ref, wbp_ref, wbs_ref, wbx_ref, wo_ref, g_ref, b_ref, *refs,
                     alpha, n_first_tiles, gate_col0):
    n_cast = (len(refs) - 1) // 2
    o_ref = refs[n_cast]
    _cast_blocks(refs[:n_cast], refs[n_cast + 1:])
    d = x_ref.shape[1]
    in_first = pl.program_id(0) < n_first_tiles

    def finish(rows, z):
        o_ref[rows, :] = _layer_norm(z, g_ref[...], b_ref[...])

    pending = None
    for r0 in range(0, x_ref.shape[0], SUB_TILE):
        rows = slice(r0, r0 + SUB_TILE)
        x = x_ref[rows, :]
        xb = x.astype(BF16)
        y = jnp.where(in_first, yp_ref[rows, :], ys_ref[rows, :])
        c0 = 0
        merged = None
        for b, w_ref in enumerate((wbp_ref, wbs_ref, wbx_ref)):
            w = w_ref.shape[0]
            gate = jax.nn.sigmoid(_dot(xb, win_ref[:, gate_col0 + b * d:gate_col0 + (b + 1) * d]))
            term = gate * _dot(y[:, c0:c0 + w], w_ref[...])
            merged = term if merged is None else merged + term
            c0 += w
            if b == 0 and pending is not None:
                finish(*pending)
                pending = None
        pending = (rows, alpha * x + _dot(merged.astype(BF16), wo_ref[...]))
    finish(*pending)


def _merge_ln(x, y_p, y_s, w_in, w_br_pool, w_br_swa, w_br_cross, w_o, ln_g, ln_b, layer, alpha, gate_col0,
              cast_weights, cast_layers):
    n, d = x.shape
    tm = WIDE_TOKEN_TILE
    n_first_tiles = y_p.shape[0] // tm
    first, second = _split_rows(n_first_tiles)
    lsel = lambda i: (layer, 0, 0)
    res = lambda w: _resident((None,) + w.shape[1:], lsel)
    cast_in, cast_out, cast_shape = _cast_specs(cast_weights, cast_layers,
                                                lambda i: jnp.minimum(i, N_CAST_BLOCKS - 1))
    assert n // tm >= N_CAST_BLOCKS
    out = pl.pallas_call(
        functools.partial(_merge_ln_kernel, alpha=alpha, n_first_tiles=n_first_tiles, gate_col0=gate_col0),
        grid=(n // tm,),
        in_specs=[
            pl.BlockSpec((tm, d), lambda i: (i, 0)),
            pl.BlockSpec((tm, y_p.shape[1]), first),
            pl.BlockSpec((tm, y_s.shape[1]), second),
            _resident(w_in.shape, lambda i: (0, 0)),
            res(w_br_pool), res(w_br_swa), res(w_br_cross), res(w_o), res(ln_g), res(ln_b),
        ] + cast_in,
        out_specs=[pl.BlockSpec((tm, d), lambda i: (i, 0))] + cast_out,
        out_shape=[jax.ShapeDtypeStruct((n, d), F32)] + cast_shape,
        compiler_params=_cparams("arbitrary"),
        name="merge_ln",
    )(x, y_p, y_s, w_in, w_br_pool, w_br_swa, w_br_cross, w_o, ln_g, ln_b, *cast_weights)
    return out[0], tuple(out[1:])


def kernel(x_prompt, x_sample, cache_swa_k, cache_swa_v, state_pool, cache_mem_k, cache_mem_v, mem_prompt, w_mem_k, w_mem_v, ffn1_w_gate, ffn1_w_up, ffn1_w_down, ln1_g, ln1_b, w_in, pool_mix, pool_scale, attn_sink, w_br_pool, w_br_swa, w_br_cross, w_o, ln2_g, ln2_b, ffn2_w_gate, ffn2_w_up, ffn2_w_down, ln3_g, ln3_b):
    batch, seq, d = x_prompt.shape
    dec_batch, dec_seq, _ = x_sample.shape
    depth = w_in.shape[0]
    n_groups, pool_group = pool_mix.shape[1], pool_mix.shape[2]
    pool_w = n_groups * pool_group
    swa_w = w_br_swa.shape[1]
    kv_w = cache_swa_k.shape[3] * cache_swa_k.shape[4]
    x_w = w_br_cross.shape[1]
    n_mem = mem_prompt.shape[1]
    widths = (pool_w, swa_w, kv_w, x_w)
    proj_w = sum(widths) + kv_w
    assert pool_group == POOL_GROUP and swa_w == N_Q_HEADS * HEAD_DIM and kv_w == N_KV_HEADS * HEAD_DIM
    assert x_w == N_X_HEADS * HEAD_DIM and cache_swa_k.shape[2] == WINDOW
    assert state_pool.shape[2] == POOL_HALO - 1 and w_in.shape[2] == proj_w + 3 * d
    alpha = (2 * depth) ** 0.25
    n_prompt, n_sample = batch * seq, dec_batch * dec_seq

    bf = lambda w: w.astype(BF16)
    vec = lambda p: p.reshape(depth, 1, p.shape[-1])
    pool_bd = bf(jnp.einsum("lgcd,gh->lgchd", pool_mix, jnp.eye(n_groups, dtype=F32))
                 .reshape(depth, pool_w, pool_w))
    state16 = jnp.pad(state_pool, ((0, 0), (0, 0), (1, 0), (0, 0)))
    to_cols = lambda c: c.transpose(0, 1, 3, 4, 2).reshape(depth, dec_batch, c.shape[3] * HEAD_DIM, c.shape[2])
    from_cols = lambda c, n_h: c.reshape(depth, dec_batch, n_h, HEAD_DIM, c.shape[-1]).transpose(0, 1, 4, 2, 3)
    cache_kt, cache_vt = to_cols(cache_swa_k), to_cols(cache_swa_v)
    cache_mkt, cache_mvt = to_cols(cache_mem_k), to_cols(cache_mem_v)

    tables = _rope_tables(seq, n_sample, dec_seq)
    mk, mv, mkh, mvt = _mem_kv(bf(mem_prompt.reshape(batch * n_mem, d)), bf(w_mem_k), bf(w_mem_v), batch)

    ffn1_f32 = (ffn1_w_gate, ffn1_w_up, ffn1_w_down)
    ffn2_f32 = (ffn2_w_gate, ffn2_w_up, ffn2_w_down)
    ffn1_w = tuple(bf(w[0]) for w in ffn1_f32)
    w_in_l = bf(w_in[0])
    ffn1_ln = (vec(ln1_g), vec(ln1_b))
    ffn2_ln = (vec(ln3_g), vec(ln3_b))
    merge = (bf(w_br_pool), bf(w_br_swa), bf(w_br_cross), bf(w_o), vec(ln2_g), vec(ln2_b))
    pscale = vec(pool_scale)

    xs = (x_prompt.reshape(n_prompt, d), x_sample.reshape(n_sample, d))
    pk, pv, pp, sp = [], [], [], []
    new_windows = (jnp.zeros(cache_kt.shape, F32), jnp.zeros(cache_vt.shape, F32))
    for l in range(depth):
        more = l + 1 < depth
        x, xp, q, k, v, xq = _ffn(xs, *ffn1_w, *ffn1_ln, l, alpha, n_prompt, proj=(w_in_l, tables, seq, widths))
        y_p, nk, nv, npool_p, ffn1_w = _mix_prompt(attn_sink, xp, q, k, v, xq, mkh, mvt, pool_bd, pscale, l,
                                                   batch, seq, cast_weights=ffn1_f32 if more else (),
                                                   cast_layer=l + 1)
        y_s, *new_windows, npool_s = _mix_sample(attn_sink, xp, q, k, v, xq, cache_kt, cache_vt, state16,
                                                 cache_mkt, cache_mvt, pool_bd, pscale, l, n_prompt, dec_seq,
                                                 new_windows)
        x, casts = _merge_ln(x, y_p, y_s, w_in_l, *merge, l, alpha, proj_w,
                             cast_weights=ffn2_f32 + ((w_in,) if more else ()), cast_layers=(l, l, l, l + 1))
        w_in_l = casts[3] if more else None
        if more:
            xs = _ffn((x,), *casts[:3], *ffn2_ln, l, alpha, n_prompt)
        else:
            n_p, n_s = n_prompt // WIDE_TOKEN_TILE, n_sample // WIDE_TOKEN_TILE
            xs = (_ffn((x,), *casts[:3], *ffn2_ln, l, alpha, n_prompt, tiles=(0, n_p))[0],
                  _ffn((x,), *casts[:3], *ffn2_ln, l, alpha, n_prompt, tiles=(n_p, n_s))[0])
        pk.append(nk)
        pv.append(nv)
        pp.append(npool_p[:, 1:])
        sp.append(npool_s[:, 1:])

    heads = lambda a, n_h: a.reshape(a.shape[:-1] + (n_h, HEAD_DIM))
    return (xs[0].reshape(batch, seq, d), xs[1].reshape(dec_batch, dec_seq, d),
            heads(jnp.stack(pk), N_KV_HEADS), heads(jnp.stack(pv), N_KV_HEADS), jnp.stack(pp),
            heads(mk.reshape(depth, batch, n_mem, x_w), N_X_HEADS),
            heads(mv.reshape(depth, batch, n_mem, x_w), N_X_HEADS),
            from_cols(new_windows[0], N_KV_HEADS), from_cols(new_windows[1], N_KV_HEADS), jnp.stack(sp))
```
